```python
import jax, jax.numpy as jnp
from jax import lax
import numpy as np

D_MODEL = 1024
BATCH = 16
SEQ = 4096
DEPTH = 4

N_BRANCHES = 4
BRANCH_WIDTH = 512
RMS_EPS = 1e-6
S5_GROUP = 16
S5_GROUPS = BRANCH_WIDTH // S5_GROUP
S5_STATE = 64
S5_DT_MIN = 1e-3
S5_DT_MAX = 1e-1
POOL_WINDOWS = (2, 4, 8, 16)
POOL_GROUP = BRANCH_WIDTH // len(POOL_WINDOWS)
SCONV_WIDTH = 3
MLSTM_HEADS = 4
MLSTM_HEAD_DIM = BRANCH_WIDTH // MLSTM_HEADS
MLSTM_CONV_WIDTH = 4
MLSTM_CHUNK = 64
MLSTM_F_BIAS_LO = 3.0
MLSTM_F_BIAS_HI = 6.0

IN_SIZES = (BRANCH_WIDTH, BRANCH_WIDTH,
            BRANCH_WIDTH, BRANCH_WIDTH,
            BRANCH_WIDTH, BRANCH_WIDTH, BRANCH_WIDTH, BRANCH_WIDTH,
            2 * BRANCH_WIDTH, BRANCH_WIDTH, BRANCH_WIDTH, MLSTM_HEADS, MLSTM_HEADS, BRANCH_WIDTH,
            N_BRANCHES * D_MODEL)
N_IN = sum(IN_SIZES)

kernel_name = 'hybrid_s5_pool_sconv_mlstm_gated_parallel'


def rms_norm(x, w):
    x32 = x.astype(jnp.float32)
    y = x32 * lax.rsqrt(jnp.mean(x32 * x32, axis=-1, keepdims=True) + RMS_EPS)
    return (y * w.astype(jnp.float32)).astype(x.dtype)


def causal_depthwise_conv(x, w):
    k_width = w.shape[0]
    s = x.shape[1]
    xp = jnp.pad(x, ((0, 0), (k_width - 1, 0), (0, 0)))
    return sum(xp[:, k:k + s, :] * w[k] for k in range(k_width))


def s5_ssm(u, a_re, a_im, log_dt, b_re, b_im, c_re, c_im, d_skip):
    bsz, s, _ = u.shape
    f32 = jnp.float32
    u32 = u.astype(f32)
    lam = lax.complex(a_re.astype(f32), a_im.astype(f32))
    dt = jnp.exp(log_dt.astype(f32))[:, None]
    a_bar = jnp.exp(lam * dt)
    b_mat = lax.complex(b_re.astype(f32), b_im.astype(f32))
    b_bar = ((a_bar - 1.0) / lam)[..., None] * b_mat
    c_mat = lax.complex(c_re.astype(f32), c_im.astype(f32))
    ug = u32.reshape(bsz, s, S5_GROUPS, S5_GROUP).astype(jnp.complex64)
    bu = jnp.einsum('bsgp,gnp->bsgn', ug, b_bar)
    a_seq = jnp.broadcast_to(a_bar, (s, S5_GROUPS, S5_STATE))

    def combine(left, right):
        a_l, x_l = left
        a_r, x_r = right
        return a_r * a_l, a_r * x_l + x_r

    def scan_one(bu_b):
        return lax.associative_scan(combine, (a_seq, bu_b), axis=0)[1]

    states = jax.vmap(scan_one)(bu)
    y = jnp.einsum('bsgn,gpn->bsgp', states, c_mat).real.reshape(bsz, s, BRANCH_WIDTH)
    y = y + d_skip.astype(f32) * u32
    return y.astype(u.dtype)


def pool_mixer(u, pool_w, pool_scale):
    bsz, s, _ = u.shape
    u32 = u.astype(jnp.float32).reshape(bsz, s, len(POOL_WINDOWS), POOL_GROUP)
    cs = jnp.cumsum(u32, axis=1)
    t = jnp.arange(1, s + 1, dtype=jnp.float32)[None, :, None]
    outs = []
    for gi, win in enumerate(POOL_WINDOWS):
        cs_g = cs[:, :, gi]
        prev = jnp.pad(cs_g, ((0, 0), (win, 0), (0, 0)))[:, :s]
        outs.append((cs_g - prev) / jnp.minimum(t, win) - u32[:, :, gi])
    pooled = jnp.stack(outs, axis=2)
    mixed = jnp.einsum('bsgc,gcd->bsgd', pooled, pool_w.astype(jnp.float32))
    return (mixed.reshape(bsz, s, BRANCH_WIDTH) * pool_scale.astype(jnp.float32)).astype(u.dtype)


def mlstm_mixer(qk, v, o_pre, i_pre, f_pre, conv_w, b_i, b_f, norm_w):
    bsz, s, _ = v.shape
    hh, dh, ll = MLSTM_HEADS, MLSTM_HEAD_DIM, MLSTM_CHUNK
    nc = s // ll
    f32 = jnp.float32
    qk = jax.nn.silu(causal_depthwise_conv(qk, conv_w)).astype(f32)
    q, k = jnp.split(qk, 2, axis=-1)
    q = q * (dh ** -0.5)

    def chunk_heads(t):
        return t.astype(f32).reshape(bsz, nc, ll, hh, dh).transpose(0, 3, 1, 2, 4)

    def chunk_gate(t):
        return t.astype(f32).reshape(bsz, nc, ll, hh).transpose(0, 3, 1, 2)

    qc, kc, vc = chunk_heads(q), chunk_heads(k), chunk_heads(v)
    i_log = chunk_gate(i_pre + b_i)
    log_f = jax.nn.log_sigmoid(chunk_gate(f_pre + b_f))
    b = jnp.cumsum(log_f, axis=-1)
    g = b[..., -1]

    a = g[..., None] - b + i_log
    m_loc = jnp.max(a, axis=-1)
    w_loc = jnp.exp(a - m_loc[..., None])
    c_loc = jnp.einsum('bhcl,bhcld,bhcle->bhcde', w_loc, kc, vc)
    n_loc = jnp.einsum('bhcl,bhcld->bhcd', w_loc, kc)

    def step(carry, inp):
        c_st, n_st, m_st = carry
        g_c, m_c, c_c, n_c = inp
        m_new = jnp.maximum(g_c + m_st, m_c)
        s_old = jnp.exp(g_c + m_st - m_new)
        s_new = jnp.exp(m_c - m_new)
        c_next = s_old[..., None, None] * c_st + s_new[..., None, None] * c_c
        n_next = s_old[..., None] * n_st + s_new[..., None] * n_c
        return (c_next, n_next, m_new), (c_st, n_st, m_st)

    init = (jnp.zeros((bsz, hh, dh, dh), f32), jnp.zeros((bsz, hh, dh), f32), jnp.zeros((bsz, hh), f32))
    xs = (jnp.moveaxis(g, 2, 0), jnp.moveaxis(m_loc, 2, 0), jnp.moveaxis(c_loc, 2, 0), jnp.moveaxis(n_loc, 2, 0))
    _, (c_prev, n_prev, m_prev) = lax.scan(step, init, xs)
    c_prev = jnp.moveaxis(c_prev, 0, 2)
    n_prev = jnp.moveaxis(n_prev, 0, 2)
    m_prev = jnp.moveaxis(m_prev, 0, 2)

    causal = jnp.tril(jnp.ones((ll, ll), dtype=bool))
    d_log = jnp.where(causal, b[..., :, None] - b[..., None, :] + i_log[..., None, :], -jnp.inf)
    e_log = b + m_prev[..., None]
    m_t = jnp.maximum(jnp.max(d_log, axis=-1), e_log)
    w_intra = jnp.exp(d_log - m_t[..., None]) * jnp.einsum('bhcld,bhcmd->bhclm', qc, kc)
    s_inter = jnp.exp(e_log - m_t)
    num = (jnp.einsum('bhclm,bhcme->bhcle', w_intra, vc)
           + s_inter[..., None] * jnp.einsum('bhcld,bhcde->bhcle', qc, c_prev))
    den = jnp.sum(w_intra, axis=-1) + s_inter * jnp.einsum('bhcld,bhcd->bhcl', qc, n_prev)
    h = num / jnp.maximum(jnp.abs(den), jnp.exp(-m_t))[..., None]
    h = h.transpose(0, 2, 3, 1, 4).reshape(bsz, s, hh, dh)
    h = h * jax.nn.sigmoid(o_pre.astype(f32)).reshape(bsz, s, hh, dh)
    h = h * lax.rsqrt(jnp.mean(h * h, axis=-1, keepdims=True) + RMS_EPS)
    return (h.reshape(bsz, s, BRANCH_WIDTH) * norm_w.astype(f32)).astype(v.dtype)


def setup_inputs(seed: int = 0) -> dict:
    key = jax.random.key(seed)
    ks = jax.random.split(key, 22)
    f32 = jnp.float32
    W, G, N, P, H = BRANCH_WIDTH, S5_GROUPS, S5_STATE, S5_GROUP, MLSTM_HEADS
    nrm = lambda k, shape: jax.random.normal(k, shape, f32)
    x = nrm(ks[0], (BATCH, SEQ, D_MODEL))
    norm_pre_w = 1.0 + 0.02 * nrm(ks[1], (DEPTH, D_MODEL))
    norm_post_w = 1.0 + 0.02 * nrm(ks[2], (DEPTH, D_MODEL))
    w_in = nrm(ks[3], (DEPTH, D_MODEL, N_IN)) * D_MODEL ** -0.5
    s5_A_re = -0.5 + 0.01 * nrm(ks[4], (DEPTH, G, N))
    s5_A_im = jnp.pi * jnp.arange(N, dtype=f32)[None, None, :] + 0.01 * nrm(ks[5], (DEPTH, G, N))
    s5_log_dt = jax.random.uniform(ks[6], (DEPTH, G), f32, float(np.log(S5_DT_MIN)), float(np.log(S5_DT_MAX)))
    s5_B_re = nrm(ks[7], (DEPTH, G, N, P)) * (2 * P) ** -0.5
    s5_B_im = nrm(ks[8], (DEPTH, G, N, P)) * (2 * P) ** -0.5
    s5_C_re = nrm(ks[9], (DEPTH, G, P, N)) * N ** -0.5
    s5_C_im = nrm(ks[10], (DEPTH, G, P, N)) * N ** -0.5
    s5_D = nrm(ks[11], (DEPTH, W))
    s5_w_glu = nrm(ks[12], (DEPTH, W, W)) * W ** -0.5
    pool_w = nrm(ks[13], (DEPTH, len(POOL_WINDOWS), POOL_GROUP, POOL_GROUP)) * POOL_GROUP ** -0.5
    pool_scale = 1.0 + 0.02 * nrm(ks[14], (DEPTH, W))
    sconv_w = nrm(ks[15], (DEPTH, SCONV_WIDTH, W)) * SCONV_WIDTH ** -0.5
    mlstm_conv_w = nrm(ks[16], (DEPTH, MLSTM_CONV_WIDTH, 2 * W)) * MLSTM_CONV_WIDTH ** -0.5
    mlstm_b_i = 0.1 * nrm(ks[17], (DEPTH, H))
    mlstm_b_f = jnp.linspace(MLSTM_F_BIAS_LO, MLSTM_F_BIAS_HI, H, dtype=f32)[None, :] + 0.01 * nrm(ks[18], (DEPTH, H))
    mlstm_norm_w = 1.0 + 0.02 * nrm(ks[19], (DEPTH, W))
    w_branch = nrm(ks[20], (DEPTH, N_BRANCHES, W, D_MODEL)) * W ** -0.5
    w_out = nrm(ks[21], (DEPTH, D_MODEL, D_MODEL)) * D_MODEL ** -0.5
    return {'x': x, 'norm_pre_w': norm_pre_w, 'norm_post_w': norm_post_w, 'w_in': w_in,
            's5_A_re': s5_A_re, 's5_A_im': s5_A_im, 's5_log_dt': s5_log_dt,
            's5_B_re': s5_B_re, 's5_B_im': s5_B_im, 's5_C_re': s5_C_re, 's5_C_im': s5_C_im,
            's5_D': s5_D, 's5_w_glu': s5_w_glu, 'pool_w': pool_w, 'pool_scale': pool_scale,
            'sconv_w': sconv_w, 'mlstm_conv_w': mlstm_conv_w, 'mlstm_b_i': mlstm_b_i,
            'mlstm_b_f': mlstm_b_f, 'mlstm_norm_w': mlstm_norm_w, 'w_branch': w_branch, 'w_out': w_out}


def reference(x, norm_pre_w, norm_post_w, w_in, s5_A_re, s5_A_im, s5_log_dt, s5_B_re, s5_B_im,
              s5_C_re, s5_C_im, s5_D, s5_w_glu, pool_w, pool_scale, sconv_w, mlstm_conv_w,
              mlstm_b_i, mlstm_b_f, mlstm_norm_w, w_branch, w_out):
    bsz, s, _ = x.shape
    split_points = np.cumsum(IN_SIZES)[:-1].tolist()
    for l in range(DEPTH):
        h = rms_norm(x, norm_pre_w[l])
        proj = h @ w_in[l]
        (s5_u, s5_z, pool_u, pool_z, sc_x, sc_b, sc_c, sc_z,
         ml_qk, ml_v, ml_o, ml_i, ml_f, ml_z, gate_pre) = jnp.split(proj, split_points, axis=-1)
        y_a = s5_ssm(s5_u, s5_A_re[l], s5_A_im[l], s5_log_dt[l], s5_B_re[l], s5_B_im[l],
                     s5_C_re[l], s5_C_im[l], s5_D[l])
        y_a = jax.nn.gelu(y_a)
        y_a = y_a * jax.nn.sigmoid(y_a @ s5_w_glu[l])
        y_b = pool_mixer(pool_u, pool_w[l], pool_scale[l])
        y_c = sc_b * causal_depthwise_conv(sc_c * sc_x, sconv_w[l])
        y_d = mlstm_mixer(ml_qk, ml_v, ml_o, ml_i, ml_f, mlstm_conv_w[l], mlstm_b_i[l],
                          mlstm_b_f[l], mlstm_norm_w[l])
        gates = jax.nn.sigmoid(gate_pre).reshape(bsz, s, N_BRANCHES, D_MODEL)
        branches = ((y_a, s5_z), (y_b, pool_z), (y_c, sc_z), (y_d, ml_z))
        merged = sum(gates[:, :, bi] * ((y * jax.nn.silu(z)) @ w_branch[l, bi])
                     for bi, (y, z) in enumerate(branches))
        out = merged @ w_out[l]
        x = x + rms_norm(out, norm_post_w[l])
    return x
```

```python
import functools

import jax
import jax.numpy as jnp
import numpy as np
from jax import lax
from jax.experimental import pallas as pl
from jax.experimental.pallas import tpu as pltpu

F32 = jnp.float32
BF16 = jnp.bfloat16

D_MODEL = 1024
N_BRANCHES = 4
WIDTH = 512
RMS_EPS = 1e-6
S5_GROUP = 16
S5_GROUPS = WIDTH // S5_GROUP
S5_STATE = 64
S5_HALF_GROUPS = 16
S5_HALF_IN = S5_HALF_GROUPS * S5_GROUP
S5_HALF_STATE = S5_HALF_GROUPS * S5_STATE
POOL_WINDOWS = (2, 4, 8, 16)
POOL_GROUP = WIDTH // len(POOL_WINDOWS)
POOL_HALO = 16
SCONV_WIDTH = 3
CONV_HALO = 8
MLSTM_HEADS = 4
MLSTM_HEAD_DIM = WIDTH // MLSTM_HEADS
MLSTM_CONV_WIDTH = 4
MLSTM_CHUNK = 128
GATE_LANES = 128

IN_SIZES = (WIDTH, WIDTH, WIDTH, WIDTH, WIDTH, WIDTH, WIDTH, WIDTH,
            2 * WIDTH, WIDTH, WIDTH, MLSTM_HEADS, MLSTM_HEADS, WIDTH, N_BRANCHES * D_MODEL)
_OFF = np.concatenate([[0], np.cumsum(IN_SIZES)]).tolist()
(O_S5U, O_S5Z, O_PU, O_PZ, O_SCX, O_SCB, O_SCC, O_SCZ,
 O_QK, O_V, O_O, O_I, O_F, O_Z, O_GATE, O_END) = _OFF

VMEM_LIMIT_BYTES = 56 * 1024 * 1024


def _rms(x, w):
    ms = jnp.mean(x * x, axis=-1, keepdims=True)
    return x * lax.rsqrt(ms + RMS_EPS) * w


def _sigmoid(x):
    return 1.0 / (1.0 + jnp.exp(-x))


def _silu(x):
    return x * _sigmoid(x)


def _gelu_tanh(x):
    c = np.float32(np.sqrt(2.0 / np.pi))
    return x * (0.5 * (1.0 + jnp.tanh(c * (x + 0.044715 * (x * x * x)))))


def _log_sigmoid(x):
    return jnp.minimum(x, 0.0) - jnp.log1p(jnp.exp(-jnp.abs(x)))


def _dot(a, b):
    return jnp.dot(a, b, preferred_element_type=F32)


def _const_spec(shape):
    nd = len(shape)
    return pl.BlockSpec(shape, lambda *_: (0,) * nd, pipeline_mode=pl.Buffered(1))


def _s5_kernel(x_ref, npre_ref, wuz_ref, wg_ref, bblk_ref, cblk_ref, are_ref, aim_ref,
               dskip_ref, wglu_ref, wb_ref, out_ref, bu_ref, st_ref, *, nb, tt, scan_lanes):
    r = nb * tt

    @pl.when(pl.program_id(0) == 0)
    def _():
        st_ref[...] = jnp.zeros_like(st_ref)

    x = x_ref[...].reshape(r, D_MODEL)
    hb = _rms(x, npre_ref[...]).astype(BF16)
    uz = _dot(hb, wuz_ref[...])
    u = uz[:, :WIDTH]
    z = uz[:, WIDTH:]
    ub = u.astype(BF16)
    for half in range(2):
        bu_ref[:, half * 2 * S5_HALF_STATE:(half + 1) * 2 * S5_HALF_STATE] = _dot(
            ub[:, half * S5_HALF_IN:(half + 1) * S5_HALF_IN], bblk_ref[half])

    for half in range(2):
        for c0 in range(0, S5_HALF_STATE, scan_lanes):
            re0 = half * 2 * S5_HALF_STATE + c0
            im0 = re0 + S5_HALF_STATE
            s0 = half * S5_HALF_STATE + c0
            ar = jnp.broadcast_to(are_ref[:, s0:s0 + scan_lanes], (nb, scan_lanes))
            ai = jnp.broadcast_to(aim_ref[:, s0:s0 + scan_lanes], (nb, scan_lanes))

            def body(t, carry, re0=re0, im0=im0, ar=ar, ai=ai):
                xr, xi = carry
                row = pl.multiple_of(t * nb, nb)
                br = bu_ref[pl.ds(row, nb), re0:re0 + scan_lanes]
                bi = bu_ref[pl.ds(row, nb), im0:im0 + scan_lanes]
                nr = ar * xr - ai * xi + br
                ni = ar * xi + ai * xr + bi
                bu_ref[pl.ds(row, nb), re0:re0 + scan_lanes] = nr
                bu_ref[pl.ds(row, nb), im0:im0 + scan_lanes] = ni
                return nr, ni

            xr, xi = lax.fori_loop(
                0, tt, body,
                (st_ref[:, re0:re0 + scan_lanes], st_ref[:, im0:im0 + scan_lanes]), unroll=4)
            st_ref[:, re0:re0 + scan_lanes] = xr
            st_ref[:, im0:im0 + scan_lanes] = xi

    ys = []
    for half in range(2):
        xs = bu_ref[:, half * 2 * S5_HALF_STATE:(half + 1) * 2 * S5_HALF_STATE].astype(BF16)
        ys.append(_dot(xs, cblk_ref[half]))
    y = jnp.concatenate(ys, axis=1) + dskip_ref[...] * u
    y = _gelu_tanh(y)
    y = y * _sigmoid(_dot(y.astype(BF16), wglu_ref[...]))
    pb = _dot((y * _silu(z)).astype(BF16), wb_ref[...])
    gate = _sigmoid(_dot(hb, wg_ref[...]))
    out_ref[...] = (gate * pb).reshape(tt, nb, D_MODEL)


def _s5_call(x_tb, npre, wuz, wg, bblk, cblk, are, aim, dskip, wglu, wb, *, tt, scan_lanes=512):
    s, nb, _ = x_tb.shape
    assert s % tt == 0 and nb % 8 == 0
    r = nb * tt
    kern = functools.partial(_s5_kernel, nb=nb, tt=tt, scan_lanes=scan_lanes)
    row_spec = pl.BlockSpec((tt, nb, D_MODEL), lambda i: (i, 0, 0))
    consts = (npre, wuz, wg, bblk, cblk, are, aim, dskip, wglu, wb)
    return pl.pallas_call(
        kern,
        out_shape=jax.ShapeDtypeStruct(x_tb.shape, F32),
        grid=(s // tt,),
        in_specs=[row_spec] + [_const_spec(c.shape) for c in consts],
        out_specs=row_spec,
        scratch_shapes=[pltpu.VMEM((r, 4 * S5_HALF_STATE), F32),
                        pltpu.VMEM((nb, 4 * S5_HALF_STATE), F32)],
        compiler_params=pltpu.CompilerParams(
            dimension_semantics=("arbitrary",), vmem_limit_bytes=VMEM_LIMIT_BYTES),
        name="s5_branch",
    )(x_tb, *consts)


def _pool_sconv_kernel(x_ref, acc_ref, npre_ref, wp_ref, wg_ref, poolw_ref, pscale_ref, scw_ref,
                       wb_ref, out_ref, pu_ref, cx_ref, *, tm):
    s_idx = pl.program_id(1)

    @pl.when(s_idx == 0)
    def _():
        pu_ref[0:POOL_HALO, :] = jnp.zeros((POOL_HALO, WIDTH), F32)
        cx_ref[0:CONV_HALO, :] = jnp.zeros((CONV_HALO, WIDTH), F32)

    @pl.when(s_idx > 0)
    def _():
        pu_ref[0:POOL_HALO, :] = pu_ref[tm:tm + POOL_HALO, :]
        cx_ref[0:CONV_HALO, :] = cx_ref[tm:tm + CONV_HALO, :]

    hb = _rms(x_ref[0], npre_ref[...]).astype(BF16)
    proj = _dot(hb, wp_ref[...])
    pool_u = proj[:, 0 * WIDTH:1 * WIDTH]
    pool_z = proj[:, 1 * WIDTH:2 * WIDTH]
    sc_x = proj[:, 2 * WIDTH:3 * WIDTH]
    sc_b = proj[:, 3 * WIDTH:4 * WIDTH]
    sc_c = proj[:, 4 * WIDTH:5 * WIDTH]
    sc_z = proj[:, 5 * WIDTH:6 * WIDTH]

    pu_ref[POOL_HALO:POOL_HALO + tm, :] = pool_u
    t_glob = s_idx * tm + lax.broadcasted_iota(jnp.int32, (tm, 1), 0)
    mixed = []
    for gi, win in enumerate(POOL_WINDOWS):
        lanes = slice(gi * POOL_GROUP, (gi + 1) * POOL_GROUP)
        wsum = pu_ref[POOL_HALO:POOL_HALO + tm, lanes]
        for j in range(1, win):
            wsum = wsum + pu_ref[POOL_HALO - j:POOL_HALO - j + tm, lanes]
        cnt = jnp.minimum(t_glob + 1, win).astype(F32)
        pooled = wsum / cnt - pool_u[:, lanes]
        mixed.append(_dot(pooled.astype(BF16), poolw_ref[gi]))
    y_b = jnp.concatenate(mixed, axis=1) * pscale_ref[...]

    cx = sc_c * sc_x
    cx_ref[CONV_HALO:CONV_HALO + tm, :] = cx
    conv = scw_ref[SCONV_WIDTH - 1:SCONV_WIDTH, :] * cx
    for k in range(SCONV_WIDTH - 1):
        sh = SCONV_WIDTH - 1 - k
        conv = conv + scw_ref[k:k + 1, :] * cx_ref[CONV_HALO - sh:CONV_HALO - sh + tm, :]
    y_c = sc_b * conv

    p_b = _dot((y_b * _silu(pool_z)).astype(BF16), wb_ref[0])
    p_c = _dot((y_c * _silu(sc_z)).astype(BF16), wb_ref[1])
    gates = _sigmoid(_dot(hb, wg_ref[...]))
    out_ref[0] = acc_ref[0] + gates[:, :D_MODEL] * p_b + gates[:, D_MODEL:] * p_c


def _pool_sconv_call(x, acc, npre, wp, wg, poolw, pscale, scw, wb, *, tm):
    bsz, s, _ = x.shape
    assert s % tm == 0 and tm % 8 == 0 and tm >= POOL_HALO
    kern = functools.partial(_pool_sconv_kernel, tm=tm)
    row_spec = pl.BlockSpec((1, tm, D_MODEL), lambda b, i: (b, i, 0))
    consts = (npre, wp, wg, poolw, pscale, scw, wb)
    return pl.pallas_call(
        kern,
        out_shape=jax.ShapeDtypeStruct(x.shape, F32),
        grid=(bsz, s // tm),
        in_specs=[row_spec, row_spec] + [_const_spec(c.shape) for c in consts],
        out_specs=row_spec,
        scratch_shapes=[pltpu.VMEM((POOL_HALO + tm, WIDTH), F32),
                        pltpu.VMEM((CONV_HALO + tm, WIDTH), F32)],
        compiler_params=pltpu.CompilerParams(
            dimension_semantics=("arbitrary", "arbitrary"), vmem_limit_bytes=VMEM_LIMIT_BYTES),
        name="pool_sconv_branch",
    )(x, acc, *consts)


def _mlstm_kernel(x_ref, acc_ref, npre_ref, npost_ref, wc_ref, wg_ref, convw_ref, bi_ref, bf_ref,
                  mnorm_ref, wb_ref, wout_ref, out_ref,
                  qk_ref, h_ref, c_ref, n_ref, m_ref, *, tm):
    s_idx = pl.program_id(1)
    hh_n, dh, ll = MLSTM_HEADS, MLSTM_HEAD_DIM, MLSTM_CHUNK

    @pl.when(s_idx == 0)
    def _():
        qk_ref[0:CONV_HALO, :] = jnp.zeros((CONV_HALO, 2 * WIDTH), F32)
        c_ref[...] = jnp.zeros_like(c_ref)
        n_ref[...] = jnp.zeros_like(n_ref)
        m_ref[...] = jnp.zeros_like(m_ref)

    @pl.when(s_idx > 0)
    def _():
        qk_ref[0:CONV_HALO, :] = qk_ref[tm:tm + CONV_HALO, :]

    x = x_ref[0]
    hb = _rms(x, npre_ref[...]).astype(BF16)
    proj = _dot(hb, wc_ref[...])
    v_all = proj[:, 2 * WIDTH:3 * WIDTH]
    o_pre = proj[:, 3 * WIDTH:4 * WIDTH]
    z = proj[:, 4 * WIDTH:5 * WIDTH]
    i_log = proj[:, 5 * WIDTH:5 * WIDTH + GATE_LANES] + bi_ref[...]
    log_f = _log_sigmoid(proj[:, 5 * WIDTH + GATE_LANES:5 * WIDTH + 2 * GATE_LANES] + bf_ref[...])

    qk_ref[CONV_HALO:CONV_HALO + tm, :] = proj[:, :2 * WIDTH]
    conv = convw_ref[MLSTM_CONV_WIDTH - 1:MLSTM_CONV_WIDTH, :] * proj[:, :2 * WIDTH]
    for k in range(MLSTM_CONV_WIDTH - 1):
        sh = MLSTM_CONV_WIDTH - 1 - k
        conv = conv + convw_ref[k:k + 1, :] * qk_ref[CONV_HALO - sh:CONV_HALO - sh + tm, :]
    qk_act = _silu(conv)
    q_all = qk_act[:, :WIDTH] * np.float32(dh ** -0.5)
    k_all = qk_act[:, WIDTH:]

    row_i = lax.broadcasted_iota(jnp.int32, (ll, ll), 0)
    col_i = lax.broadcasted_iota(jnp.int32, (ll, ll), 1)
    causal = col_i <= row_i
    tri = causal.astype(BF16)

    for c in range(tm // ll):
        rows = slice(c * ll, (c + 1) * ll)
        lf_c = log_f[rows]
        lf_hi = lf_c.astype(BF16)
        lf_lo = (lf_c - lf_hi.astype(F32)).astype(BF16)
        b_cs = _dot(tri, lf_hi) + _dot(tri, lf_lo)
        r_col = i_log[rows] - b_cs
        r_row = r_col.T
        g_tot = b_cs[ll - 1:ll, :]
        a_col = g_tot + r_col
        for hh in range(hh_n):
            lanes = slice(hh * dh, (hh + 1) * dh)
            qh = q_all[rows, lanes]
            kh = k_all[rows, lanes]
            vh = v_all[rows, lanes].astype(BF16)
            qhb = qh.astype(BF16)
            m_prev = m_ref[hh][:, 0:1]
            c_prev = c_ref[hh]
            n_prev = n_ref[hh]
            b_h = b_cs[:, hh:hh + 1]
            d_log = jnp.where(causal, b_h + r_row[hh:hh + 1, :], -jnp.inf)
            e_log = b_h + m_prev
            m_t = jnp.maximum(jnp.max(d_log, axis=1, keepdims=True), e_log)
            s_qk = lax.dot_general(qhb, kh.astype(BF16), (((1,), (1,)), ((), ())),
                                   preferred_element_type=F32)
            w_intra = jnp.exp(d_log - m_t) * s_qk
            s_inter = jnp.exp(e_log - m_t)
            num = _dot(w_intra.astype(BF16), vh) + s_inter * _dot(qhb, c_prev.astype(BF16))
            den = (jnp.sum(w_intra, axis=1, keepdims=True)
                   + s_inter * jnp.sum(qh * n_prev, axis=1, keepdims=True))
            h_ref[rows, lanes] = num / jnp.maximum(jnp.abs(den), jnp.exp(-m_t))
            a_h = a_col[:, hh:hh + 1]
            m_loc = jnp.max(a_h, axis=0, keepdims=True)
            kw = jnp.exp(a_h - m_loc) * kh
            c_loc = lax.dot_general(kw.astype(BF16), vh, (((0,), (0,)), ((), ())),
                                    preferred_element_type=F32)
            n_loc = jnp.sum(kw, axis=0, keepdims=True)
            g_h = g_tot[:, hh:hh + 1]
            m_new = jnp.maximum(g_h + m_prev, m_loc)
            s_old = jnp.exp(g_h + m_prev - m_new)
            s_new = jnp.exp(m_loc - m_new)
            c_ref[hh] = s_old * c_prev + s_new * c_loc
            n_ref[hh] = s_old * n_prev + s_new * n_loc
            m_ref[hh] = jnp.broadcast_to(m_new, (1, dh))

    hg = h_ref[...] * _sigmoid(o_pre)
    normed = []
    for hh in range(hh_n):
        blk = hg[:, hh * dh:(hh + 1) * dh]
        normed.append(blk * lax.rsqrt(jnp.mean(blk * blk, axis=-1, keepdims=True) + RMS_EPS))
    y_d = jnp.concatenate(normed, axis=1) * mnorm_ref[...]

    p_d = _dot((y_d * _silu(z)).astype(BF16), wb_ref[...])
    gate = _sigmoid(_dot(hb, wg_ref[...]))
    merged = acc_ref[0] + gate * p_d
    out = _dot(merged.astype(BF16), wout_ref[...])
    out_ref[0] = x + _rms(out, npost_ref[...])


def _mlstm_call(x, acc, npre, npost, wc, wg, convw, b_i, b_f, mnorm, wb, wout, *, tm):
    bsz, s, _ = x.shape
    assert s % tm == 0 and tm % MLSTM_CHUNK == 0
    kern = functools.partial(_mlstm_kernel, tm=tm)
    row_spec = pl.BlockSpec((1, tm, D_MODEL), lambda b, i: (b, i, 0))
    consts = (npre, npost, wc, wg, convw, b_i, b_f, mnorm, wb, wout)
    return pl.pallas_call(
        kern,
        out_shape=jax.ShapeDtypeStruct(x.shape, F32),
        grid=(bsz, s // tm),
        in_specs=[row_spec, row_spec] + [_const_spec(c.shape) for c in consts],
        out_specs=row_spec,
        scratch_shapes=[pltpu.VMEM((CONV_HALO + tm, 2 * WIDTH), F32),
                        pltpu.VMEM((tm, WIDTH), F32),
                        pltpu.VMEM((MLSTM_HEADS, MLSTM_HEAD_DIM, MLSTM_HEAD_DIM), F32),
                        pltpu.VMEM((MLSTM_HEADS, 1, MLSTM_HEAD_DIM), F32),
                        pltpu.VMEM((MLSTM_HEADS, 1, MLSTM_HEAD_DIM), F32)],
        compiler_params=pltpu.CompilerParams(
            dimension_semantics=("arbitrary", "arbitrary"), vmem_limit_bytes=VMEM_LIMIT_BYTES),
        name="mlstm_merge_out",
    )(x, acc, *consts)


def _s5_params(a_re, a_im, log_dt, b_re, b_im, c_re, c_im):
    lam = lax.complex(a_re.astype(F32), a_im.astype(F32))
    dt = jnp.exp(log_dt.astype(F32))[:, None]
    a_bar = jnp.exp(lam * dt)
    b_bar = ((a_bar - 1.0) / lam)[..., None] * lax.complex(b_re.astype(F32), b_im.astype(F32))
    eye = jnp.eye(S5_HALF_GROUPS, dtype=F32)

    def block_in(m):
        return jnp.einsum('gnp,gh->gphn', m, eye).reshape(S5_HALF_IN, S5_HALF_STATE)

    def block_out(m):
        return jnp.einsum('gpn,gh->gnhp', m, eye).reshape(S5_HALF_STATE, S5_HALF_IN)

    bblk, cblk = [], []
    for half in range(2):
        gs = slice(half * S5_HALF_GROUPS, (half + 1) * S5_HALF_GROUPS)
        bblk.append(jnp.concatenate([block_in(jnp.real(b_bar[gs])), block_in(jnp.imag(b_bar[gs]))], axis=1))
        cblk.append(jnp.concatenate([block_out(c_re[gs].astype(F32)), block_out(-c_im[gs].astype(F32))], axis=0))
    bblk = jnp.stack(bblk).astype(BF16)
    cblk = jnp.stack(cblk).astype(BF16)
    are = jnp.real(a_bar).reshape(1, S5_GROUPS * S5_STATE)
    aim = jnp.imag(a_bar).reshape(1, S5_GROUPS * S5_STATE)
    return bblk, cblk, are, aim


def _pad_lanes(v, n):
    return jnp.pad(v.astype(F32), (0, n - v.shape[0])).reshape(1, n)


def _pad_rows(w, n):
    return jnp.pad(w.astype(F32), ((0, n - w.shape[0]), (0, 0)))


def _forward(x, norm_pre_w, norm_post_w, w_in, s5_A_re, s5_A_im, s5_log_dt, s5_B_re, s5_B_im,
             s5_C_re, s5_C_im, s5_D, s5_w_glu, pool_w, pool_scale, sconv_w, mlstm_conv_w,
             mlstm_b_i, mlstm_b_f, mlstm_norm_w, w_branch, w_out, *, tt, tm_ps, tm_ml):
    depth = w_in.shape[0]
    row = lambda v: v.astype(F32).reshape(1, -1)
    for l in range(depth):
        wl = w_in[l]
        gate_w = lambda bi: wl[:, O_GATE + bi * D_MODEL:O_GATE + (bi + 1) * D_MODEL]
        npre = row(norm_pre_w[l])
        bblk, cblk, are, aim = _s5_params(s5_A_re[l], s5_A_im[l], s5_log_dt[l], s5_B_re[l],
                                          s5_B_im[l], s5_C_re[l], s5_C_im[l])
        acc_tb = _s5_call(
            jnp.swapaxes(x, 0, 1), npre, wl[:, O_S5U:O_PU].astype(BF16), gate_w(0).astype(BF16),
            bblk, cblk, are, aim, row(s5_D[l]), s5_w_glu[l].astype(BF16),
            w_branch[l, 0].astype(BF16), tt=tt)
        acc = jnp.swapaxes(acc_tb, 0, 1)
        acc = _pool_sconv_call(
            x, acc, npre, wl[:, O_PU:O_QK].astype(BF16),
            jnp.concatenate([gate_w(1), gate_w(2)], axis=1).astype(BF16),
            pool_w[l].astype(BF16), row(pool_scale[l]), _pad_rows(sconv_w[l], 8),
            w_branch[l, 1:3].astype(BF16), tm=tm_ps)
        zero_gate = jnp.zeros((D_MODEL, GATE_LANES - MLSTM_HEADS), w_in.dtype)
        wc = jnp.concatenate([wl[:, O_QK:O_I], wl[:, O_Z:O_GATE], wl[:, O_I:O_F], zero_gate,
                              wl[:, O_F:O_Z], zero_gate], axis=1).astype(BF16)
        x = _mlstm_call(
            x, acc, npre, row(norm_post_w[l]), wc, gate_w(3).astype(BF16),
            _pad_rows(mlstm_conv_w[l], 8), _pad_lanes(mlstm_b_i[l], GATE_LANES),
            _pad_lanes(mlstm_b_f[l], GATE_LANES), row(mlstm_norm_w[l]),
            w_branch[l, 3].astype(BF16), w_out[l].astype(BF16), tm=tm_ml)
    return x


def kernel(x, norm_pre_w, norm_post_w, w_in, s5_A_re, s5_A_im, s5_log_dt, s5_B_re, s5_B_im, s5_C_re, s5_C_im, s5_D, s5_w_glu, pool_w, pool_scale, sconv_w, mlstm_conv_w, mlstm_b_i, mlstm_b_f, mlstm_norm_w, w_branch, w_out):
    return _forward(x, norm_pre_w, norm_post_w, w_in, s5_A_re, s5_A_im, s5_log_dt, s5_B_re,
                    s5_B_im, s5_C_re, s5_C_im, s5_D, s5_w_glu, pool_w, pool_scale, sconv_w,
                    mlstm_conv_w, mlstm_b_i, mlstm_b_f, mlstm_norm_w, w_branch, w_out,
                    tt=32, tm_ps=512, tm_ml=512)
```

```python
import functools

import jax
import jax.numpy as jnp
import numpy as np
from jax import lax
from jax.experimental import pallas as pl
from jax.experimental.pallas import tpu as pltpu

F32 = jnp.float32
BF16 = jnp.bfloat16

D_MODEL = 1024
N_BRANCHES = 4
WIDTH = 512
RMS_EPS = 1e-6
S5_GROUP = 16
S5_GROUPS = WIDTH // S5_GROUP
S5_STATE = 64
S5_HALF_GROUPS = 16
S5_HALF_IN = S5_HALF_GROUPS * S5_GROUP
S5_HALF_STATE = S5_HALF_GROUPS * S5_STATE
POOL_WINDOWS = (2, 4, 8, 16)
POOL_GROUP = WIDTH // len(POOL_WINDOWS)
POOL_HALO = 16
SCONV_WIDTH = 3
CONV_HALO = 8
MLSTM_HEADS = 4
MLSTM_HEAD_DIM = WIDTH // MLSTM_HEADS
MLSTM_CONV_WIDTH = 4
MLSTM_CHUNK = 256
LANES = 128
GATE_LANES = LANES

IN_SIZES = (WIDTH, WIDTH, WIDTH, WIDTH, WIDTH, WIDTH, WIDTH, WIDTH,
            2 * WIDTH, WIDTH, WIDTH, MLSTM_HEADS, MLSTM_HEADS, WIDTH, N_BRANCHES * D_MODEL)
_OFF = np.concatenate([[0], np.cumsum(IN_SIZES)]).tolist()
(O_S5U, O_S5Z, O_PU, O_PZ, O_SCX, O_SCB, O_SCC, O_SCZ,
 O_QK, O_V, O_O, O_I, O_F, O_Z, O_GATE, O_END) = _OFF

VMEM_LIMIT_BYTES = 56 * 1024 * 1024


def _rms(x, w):
    ms = jnp.mean(x * x, axis=-1, keepdims=True)
    return x * lax.rsqrt(ms + RMS_EPS) * w


def _sigmoid(x):
    return 1.0 / (1.0 + jnp.exp(-x))


def _silu(x):
    return x * _sigmoid(x)


def _gelu_tanh(x):
    c = np.float32(np.sqrt(2.0 / np.pi))
    return x * (0.5 * (1.0 + jnp.tanh(c * (x + 0.044715 * (x * x * x)))))


def _log_sigmoid(x):
    return jnp.minimum(x, 0.0) - jnp.log1p(jnp.exp(-jnp.abs(x)))


def _dot(a, b):
    return jnp.dot(a, b, preferred_element_type=F32)


def _const_spec(shape):
    nd = len(shape)
    return pl.BlockSpec(shape, lambda *_: (0,) * nd, pipeline_mode=pl.Buffered(1))


def _s5_kernel(x_ref, npre_ref, wuz_ref, wg_ref, bblk_ref, cblk_ref, are_ref, aim_ref,
               dskip_ref, wglu_ref, wb_ref, out_ref, bu_ref, st_ref, *, nb, tt, pitch, scan_slabs):
    r = nb * tt
    half_slabs = 2 * S5_HALF_STATE // LANES
    re_slabs = S5_HALF_STATE // LANES

    @pl.when(pl.program_id(0) == 0)
    def _():
        st_ref[...] = jnp.zeros_like(st_ref)

    x = x_ref[...].reshape(r, D_MODEL)
    hb = _rms(x, npre_ref[...]).astype(BF16)
    uz = _dot(hb, wuz_ref[...])
    u = uz[:, :WIDTH]
    z = uz[:, WIDTH:]
    ub = u.astype(BF16)
    for half in range(2):
        bu = _dot(ub[:, half * S5_HALF_IN:(half + 1) * S5_HALF_IN], bblk_ref[half])
        for b in range(nb):
            for k in range(half_slabs):
                bu_ref[half * half_slabs + k, b * pitch:b * pitch + tt, :] = (
                    bu[b * tt:(b + 1) * tt, k * LANES:(k + 1) * LANES])

    for half in range(2):
        for k0 in range(0, re_slabs, scan_slabs):
            ks = [half * half_slabs + k0 + j for j in range(scan_slabs)]
            coef = []
            for j in range(scan_slabs):
                s0 = half * S5_HALF_STATE + (k0 + j) * LANES
                coef.append((jnp.broadcast_to(are_ref[:, s0:s0 + LANES], (nb, LANES)),
                             jnp.broadcast_to(aim_ref[:, s0:s0 + LANES], (nb, LANES))))

            def body(t, carry, ks=ks, coef=coef):
                new = []
                for (xr, xi), (ar, ai), k in zip(carry, coef, ks):
                    rows = pl.ds(t, nb, stride=pitch)
                    nr = ar * xr - ai * xi + bu_ref[k, rows, :]
                    ni = ar * xi + ai * xr + bu_ref[k + re_slabs, rows, :]
                    bu_ref[k, rows, :] = nr
                    bu_ref[k + re_slabs, rows, :] = ni
                    new.append((nr, ni))
                return tuple(new)

            init = tuple((st_ref[k], st_ref[k + re_slabs]) for k in ks)
            fin = lax.fori_loop(0, tt, body, init, unroll=4)
            for (xr, xi), k in zip(fin, ks):
                st_ref[k] = xr
                st_ref[k + re_slabs] = xi

    ys = []
    for half in range(2):
        xs = jnp.concatenate(
            [jnp.concatenate([bu_ref[half * half_slabs + k, b * pitch:b * pitch + tt, :]
                              for k in range(half_slabs)], axis=1) for b in range(nb)], axis=0)
        ys.append(_dot(xs.astype(BF16), cblk_ref[half]))
    y = jnp.concatenate(ys, axis=1) + dskip_ref[...] * u
    y = _gelu_tanh(y)
    y = y * _sigmoid(_dot(y.astype(BF16), wglu_ref[...]))
    pb = _dot((y * _silu(z)).astype(BF16), wb_ref[...])
    gate = _sigmoid(_dot(hb, wg_ref[...]))
    out_ref[...] = (gate * pb).reshape(nb, tt, D_MODEL)


def _scan_pitch(tt):
    return tt if (tt // 8) % 2 == 1 else tt + 8


def _s5_call(x, npre, wuz, wg, bblk, cblk, are, aim, dskip, wglu, wb, *, tt, scan_slabs=4):
    nb, s, _ = x.shape
    assert s % tt == 0 and tt % 8 == 0 and nb % 8 == 0
    pitch = _scan_pitch(tt)
    n_slabs = 4 * S5_HALF_STATE // LANES
    kern = functools.partial(_s5_kernel, nb=nb, tt=tt, pitch=pitch, scan_slabs=scan_slabs)
    row_spec = pl.BlockSpec((nb, tt, D_MODEL), lambda i: (0, i, 0))
    consts = (npre, wuz, wg, bblk, cblk, are, aim, dskip, wglu, wb)
    return pl.pallas_call(
        kern,
        out_shape=jax.ShapeDtypeStruct(x.shape, F32),
        grid=(s // tt,),
        in_specs=[row_spec] + [_const_spec(c.shape) for c in consts],
        out_specs=row_spec,
        scratch_shapes=[pltpu.VMEM((n_slabs, nb * pitch, LANES), F32),
                        pltpu.VMEM((n_slabs, nb, LANES), F32)],
        compiler_params=pltpu.CompilerParams(
            dimension_semantics=("arbitrary",), vmem_limit_bytes=VMEM_LIMIT_BYTES),
        name="s5_branch",
    )(x, *consts)


def _pool_sconv_kernel(x_ref, acc_ref, npre_ref, wp_ref, wg_ref, poolw_ref, pscale_ref, scw_ref,
                       wb_ref, out_ref, pu_ref, cx_ref, *, tm):
    s_idx = pl.program_id(1)

    @pl.when(s_idx == 0)
    def _():
        pu_ref[0:POOL_HALO, :] = jnp.zeros((POOL_HALO, WIDTH), F32)
        cx_ref[0:CONV_HALO, :] = jnp.zeros((CONV_HALO, WIDTH), F32)

    @pl.when(s_idx > 0)
    def _():
        pu_ref[0:POOL_HALO, :] = pu_ref[tm:tm + POOL_HALO, :]
        cx_ref[0:CONV_HALO, :] = cx_ref[tm:tm + CONV_HALO, :]

    hb = _rms(x_ref[...], npre_ref[...]).astype(BF16)
    proj = _dot(hb, wp_ref[...])
    pool_u = proj[:, 0 * WIDTH:1 * WIDTH]
    pool_z = proj[:, 1 * WIDTH:2 * WIDTH]
    sc_x = proj[:, 2 * WIDTH:3 * WIDTH]
    sc_b = proj[:, 3 * WIDTH:4 * WIDTH]
    sc_c = proj[:, 4 * WIDTH:5 * WIDTH]
    sc_z = proj[:, 5 * WIDTH:6 * WIDTH]

    pu_ref[POOL_HALO:POOL_HALO + tm, :] = pool_u
    t_glob = s_idx * tm + lax.broadcasted_iota(jnp.int32, (tm, 1), 0)
    mixed = []
    for gi, win in enumerate(POOL_WINDOWS):
        lanes = slice(gi * POOL_GROUP, (gi + 1) * POOL_GROUP)
        wsum = pu_ref[POOL_HALO:POOL_HALO + tm, lanes]
        for j in range(1, win):
            wsum = wsum + pu_ref[POOL_HALO - j:POOL_HALO - j + tm, lanes]
        cnt = jnp.minimum(t_glob + 1, win).astype(F32)
        pooled = wsum / cnt - pool_u[:, lanes]
        mixed.append(_dot(pooled.astype(BF16), poolw_ref[gi]))
    y_b = jnp.concatenate(mixed, axis=1) * pscale_ref[...]

    cx = sc_c * sc_x
    cx_ref[CONV_HALO:CONV_HALO + tm, :] = cx
    conv = scw_ref[SCONV_WIDTH - 1:SCONV_WIDTH, :] * cx
    for k in range(SCONV_WIDTH - 1):
        sh = SCONV_WIDTH - 1 - k
        conv = conv + scw_ref[k:k + 1, :] * cx_ref[CONV_HALO - sh:CONV_HALO - sh + tm, :]
    y_c = sc_b * conv

    p_b = _dot((y_b * _silu(pool_z)).astype(BF16), wb_ref[0])
    p_c = _dot((y_c * _silu(sc_z)).astype(BF16), wb_ref[1])
    gates = _sigmoid(_dot(hb, wg_ref[...]))
    out_ref[...] = acc_ref[...] + gates[:, :D_MODEL] * p_b + gates[:, D_MODEL:] * p_c


def _pool_sconv_call(x, acc, npre, wp, wg, poolw, pscale, scw, wb, *, tm):
    bsz, s, _ = x.shape
    assert s % tm == 0 and tm % 8 == 0 and tm >= POOL_HALO
    kern = functools.partial(_pool_sconv_kernel, tm=tm)
    row_spec = pl.BlockSpec((None, tm, D_MODEL), lambda b, i: (b, i, 0))
    consts = (npre, wp, wg, poolw, pscale, scw, wb)
    return pl.pallas_call(
        kern,
        out_shape=jax.ShapeDtypeStruct(x.shape, F32),
        grid=(bsz, s // tm),
        in_specs=[row_spec, row_spec] + [_const_spec(c.shape) for c in consts],
        out_specs=row_spec,
        scratch_shapes=[pltpu.VMEM((POOL_HALO + tm, WIDTH), F32),
                        pltpu.VMEM((CONV_HALO + tm, WIDTH), F32)],
        compiler_params=pltpu.CompilerParams(
            dimension_semantics=("arbitrary", "arbitrary"), vmem_limit_bytes=VMEM_LIMIT_BYTES),
        name="pool_sconv_branch",
    )(x, acc, *consts)


def _mlstm_kernel(x_ref, acc_ref, npre_ref, npost_ref, wc_ref, wg_ref, convw_ref, bi_ref, bf_ref,
                  mnorm_ref, wb_ref, wout_ref, out_ref,
                  qk_ref, h_ref, c_ref, m_ref, *, tm):
    s_idx = pl.program_id(1)
    hh_n, dh, ll = MLSTM_HEADS, MLSTM_HEAD_DIM, MLSTM_CHUNK

    @pl.when(s_idx == 0)
    def _():
        qk_ref[0:CONV_HALO, :] = jnp.zeros((CONV_HALO, 2 * WIDTH), F32)
        c_ref[...] = jnp.zeros_like(c_ref)
        m_ref[...] = jnp.zeros_like(m_ref)

    @pl.when(s_idx > 0)
    def _():
        qk_ref[0:CONV_HALO, :] = qk_ref[tm:tm + CONV_HALO, :]

    x = x_ref[...]
    hb = _rms(x, npre_ref[...]).astype(BF16)
    proj = _dot(hb, wc_ref[...])
    v_all = proj[:, 2 * WIDTH:3 * WIDTH]
    o_pre = proj[:, 3 * WIDTH:4 * WIDTH]
    z = proj[:, 4 * WIDTH:5 * WIDTH]
    i_log = proj[:, 5 * WIDTH:5 * WIDTH + GATE_LANES] + bi_ref[...]
    log_f = _log_sigmoid(proj[:, 5 * WIDTH + GATE_LANES:5 * WIDTH + 2 * GATE_LANES] + bf_ref[...])

    qk_ref[CONV_HALO:CONV_HALO + tm, :] = proj[:, :2 * WIDTH]
    conv = convw_ref[MLSTM_CONV_WIDTH - 1:MLSTM_CONV_WIDTH, :] * proj[:, :2 * WIDTH]
    for k in range(MLSTM_CONV_WIDTH - 1):
        sh = MLSTM_CONV_WIDTH - 1 - k
        conv = conv + convw_ref[k:k + 1, :] * qk_ref[CONV_HALO - sh:CONV_HALO - sh + tm, :]
    qk_act = _silu(conv)
    q_all = qk_act[:, :WIDTH] * np.float32(dh ** -0.5)
    k_all = qk_act[:, WIDTH:]

    row_i = lax.broadcasted_iota(jnp.int32, (ll, ll), 0)
    col_i = lax.broadcasted_iota(jnp.int32, (ll, ll), 1)
    causal = col_i <= row_i
    tri = causal.astype(BF16)

    ones_blk = (lax.broadcasted_iota(jnp.int32, (ll, dh), 1) == 0).astype(BF16)

    pre = []
    for c in range(tm // ll):
        rows = slice(c * ll, (c + 1) * ll)
        lf_c = log_f[rows]
        lf_hi = lf_c.astype(BF16)
        lf_lo = (lf_c - lf_hi.astype(F32)).astype(BF16)
        b_cs = _dot(tri, lf_hi) + _dot(tri, lf_lo)
        r_col = i_log[rows] - b_cs
        r_row = r_col.T
        g_tot = b_cs[ll - 1:ll, :]
        a_col = g_tot + r_col
        for hh in range(hh_n):
            lanes = slice(hh * dh, (hh + 1) * dh)
            qhb = q_all[rows, lanes].astype(BF16)
            kh = k_all[rows, lanes]
            vext = jnp.concatenate([v_all[rows, lanes].astype(BF16), ones_blk], axis=1)
            b_h = b_cs[:, hh:hh + 1]
            d_log = jnp.where(causal, b_h + r_row[hh:hh + 1, :], -jnp.inf)
            m_in = jnp.max(d_log, axis=1, keepdims=True)
            s_qk = lax.dot_general(qhb, kh.astype(BF16), (((1,), (1,)), ((), ())),
                                   preferred_element_type=F32)
            w_un = jnp.exp(d_log - m_in) * s_qk
            nd_in = _dot(w_un.astype(BF16), vext)
            a_h = a_col[:, hh:hh + 1]
            m_loc = jnp.max(a_h, axis=0, keepdims=True)
            kw = jnp.exp(a_h - m_loc) * kh
            cn_loc = lax.dot_general(kw.astype(BF16), vext, (((0,), (0,)), ((), ())),
                                     preferred_element_type=F32)
            pre.append((rows, lanes, hh, qhb, b_h, m_in, nd_in, m_loc, cn_loc,
                        g_tot[:, hh:hh + 1]))

    for rows, lanes, hh, qhb, b_h, m_in, nd_in, m_loc, cn_loc, g_h in pre:
        m_prev = m_ref[hh][:, 0:1]
        cn_prev = c_ref[hh]
        e_log = b_h + m_prev
        m_t = jnp.maximum(m_in, e_log)
        tot = (jnp.exp(m_in - m_t) * nd_in
               + jnp.exp(e_log - m_t) * _dot(qhb, cn_prev.astype(BF16)))
        h_ref[rows, lanes] = tot[:, :dh] / jnp.maximum(jnp.abs(tot[:, dh:dh + 1]), jnp.exp(-m_t))
        m_new = jnp.maximum(g_h + m_prev, m_loc)
        c_ref[hh] = jnp.exp(g_h + m_prev - m_new) * cn_prev + jnp.exp(m_loc - m_new) * cn_loc
        m_ref[hh] = jnp.broadcast_to(m_new, (1, dh))

    hg = h_ref[...] * _sigmoid(o_pre)
    normed = []
    for hh in range(hh_n):
        blk = hg[:, hh * dh:(hh + 1) * dh]
        normed.append(blk * lax.rsqrt(jnp.mean(blk * blk, axis=-1, keepdims=True) + RMS_EPS))
    y_d = jnp.concatenate(normed, axis=1) * mnorm_ref[...]

    p_d = _dot((y_d * _silu(z)).astype(BF16), wb_ref[...])
    gate = _sigmoid(_dot(hb, wg_ref[...]))
    merged = acc_ref[...] + gate * p_d
    out = _dot(merged.astype(BF16), wout_ref[...])
    out_ref[...] = x + _rms(out, npost_ref[...])


def _mlstm_call(x, acc, npre, npost, wc, wg, convw, b_i, b_f, mnorm, wb, wout, *, tm):
    bsz, s, _ = x.shape
    assert s % tm == 0 and tm % MLSTM_CHUNK == 0
    kern = functools.partial(_mlstm_kernel, tm=tm)
    row_spec = pl.BlockSpec((None, tm, D_MODEL), lambda b, i: (b, i, 0))
    consts = (npre, npost, wc, wg, convw, b_i, b_f, mnorm, wb, wout)
    return pl.pallas_call(
        kern,
        out_shape=jax.ShapeDtypeStruct(x.shape, F32),
        grid=(bsz, s // tm),
        in_specs=[row_spec, row_spec] + [_const_spec(c.shape) for c in consts],
        out_specs=row_spec,
        scratch_shapes=[pltpu.VMEM((CONV_HALO + tm, 2 * WIDTH), F32),
                        pltpu.VMEM((tm, WIDTH), F32),
                        pltpu.VMEM((MLSTM_HEADS, MLSTM_HEAD_DIM, 2 * MLSTM_HEAD_DIM), F32),
                        pltpu.VMEM((MLSTM_HEADS, 1, MLSTM_HEAD_DIM), F32)],
        compiler_params=pltpu.CompilerParams(
            dimension_semantics=("arbitrary", "arbitrary"), vmem_limit_bytes=VMEM_LIMIT_BYTES),
        name="mlstm_merge_out",
    )(x, acc, *consts)


def _s5_params(a_re, a_im, log_dt, b_re, b_im, c_re, c_im):
    lam = lax.complex(a_re.astype(F32), a_im.astype(F32))
    dt = jnp.exp(log_dt.astype(F32))[:, None]
    a_bar = jnp.exp(lam * dt)
    b_bar = ((a_bar - 1.0) / lam)[..., None] * lax.complex(b_re.astype(F32), b_im.astype(F32))
    eye = jnp.eye(S5_HALF_GROUPS, dtype=F32)

    def block_in(m):
        return jnp.einsum('gnp,gh->gphn', m, eye).reshape(S5_HALF_IN, S5_HALF_STATE)

    def block_out(m):
        return jnp.einsum('gpn,gh->gnhp', m, eye).reshape(S5_HALF_STATE, S5_HALF_IN)

    bblk, cblk = [], []
    for half in range(2):
        gs = slice(half * S5_HALF_GROUPS, (half + 1) * S5_HALF_GROUPS)
        bblk.append(jnp.concatenate([block_in(jnp.real(b_bar[gs])), block_in(jnp.imag(b_bar[gs]))], axis=1))
        cblk.append(jnp.concatenate([block_out(c_re[gs].astype(F32)), block_out(-c_im[gs].astype(F32))], axis=0))
    bblk = jnp.stack(bblk).astype(BF16)
    cblk = jnp.stack(cblk).astype(BF16)
    are = jnp.real(a_bar).reshape(1, S5_GROUPS * S5_STATE)
    aim = jnp.imag(a_bar).reshape(1, S5_GROUPS * S5_STATE)
    return bblk, cblk, are, aim


def _pad_lanes(v, n):
    return jnp.pad(v.astype(F32), (0, n - v.shape[0])).reshape(1, n)


def _pad_rows(w, n):
    return jnp.pad(w.astype(F32), ((0, n - w.shape[0]), (0, 0)))


def _forward(x, norm_pre_w, norm_post_w, w_in, s5_A_re, s5_A_im, s5_log_dt, s5_B_re, s5_B_im,
             s5_C_re, s5_C_im, s5_D, s5_w_glu, pool_w, pool_scale, sconv_w, mlstm_conv_w,
             mlstm_b_i, mlstm_b_f, mlstm_norm_w, w_branch, w_out, *, tt, tm_ps, tm_ml):
    depth = w_in.shape[0]
    row = lambda v: v.astype(F32).reshape(1, -1)
    for l in range(depth):
        wl = w_in[l]
        gate_w = lambda bi: wl[:, O_GATE + bi * D_MODEL:O_GATE + (bi + 1) * D_MODEL]
        npre = row(norm_pre_w[l])
        bblk, cblk, are, aim = _s5_params(s5_A_re[l], s5_A_im[l], s5_log_dt[l], s5_B_re[l],
                                          s5_B_im[l], s5_C_re[l], s5_C_im[l])
        acc = _s5_call(
            x, npre, wl[:, O_S5U:O_PU].astype(BF16), gate_w(0).astype(BF16),
            bblk, cblk, are, aim, row(s5_D[l]), s5_w_glu[l].astype(BF16),
            w_branch[l, 0].astype(BF16), tt=tt)
        acc = _pool_sconv_call(
            x, acc, npre, wl[:, O_PU:O_QK].astype(BF16),
            jnp.concatenate([gate_w(1), gate_w(2)], axis=1).astype(BF16),
            pool_w[l].astype(BF16), row(pool_scale[l]), _pad_rows(sconv_w[l], 8),
            w_branch[l, 1:3].astype(BF16), tm=tm_ps)
        zero_gate = jnp.zeros((D_MODEL, GATE_LANES - MLSTM_HEADS), w_in.dtype)
        wc = jnp.concatenate([wl[:, O_QK:O_I], wl[:, O_Z:O_GATE], wl[:, O_I:O_F], zero_gate,
                              wl[:, O_F:O_Z], zero_gate], axis=1).astype(BF16)
        x = _mlstm_call(
            x, acc, npre, row(norm_post_w[l]), wc, gate_w(3).astype(BF16),
            _pad_rows(mlstm_conv_w[l], 8), _pad_lanes(mlstm_b_i[l], GATE_LANES),
            _pad_lanes(mlstm_b_f[l], GATE_LANES), row(mlstm_norm_w[l]),
            w_branch[l, 3].astype(BF16), w_out[l].astype(BF16), tm=tm_ml)
    return x


def kernel(x, norm_pre_w, norm_post_w, w_in, s5_A_re, s5_A_im, s5_log_dt, s5_B_re, s5_B_im, s5_C_re, s5_C_im, s5_D, s5_w_glu, pool_w, pool_scale, sconv_w, mlstm_conv_w, mlstm_b_i, mlstm_b_f, mlstm_norm_w, w_branch, w_out):
    return _forward(x, norm_pre_w, norm_post_w, w_in, s5_A_re, s5_A_im, s5_log_dt, s5_B_re,
                    s5_B_im, s5_C_re, s5_C_im, s5_D, s5_w_glu, pool_w, pool_scale, sconv_w,
                    mlstm_conv_w, mlstm_b_i, mlstm_b_f, mlstm_norm_w, w_branch, w_out,
                    tt=32, tm_ps=512, tm_ml=512)
```

```python
import functools

import jax
import jax.numpy as jnp
import numpy as np
from jax import lax
from jax.experimental import pallas as pl
from jax.experimental.pallas import tpu as pltpu

F32 = jnp.float32
BF16 = jnp.bfloat16

D_MODEL = 1024
N_BRANCHES = 4
WIDTH = 512
RMS_EPS = 1e-6
S5_GROUP = 16
S5_GROUPS = WIDTH // S5_GROUP
S5_STATE = 64
S5_HALF_GROUPS = 16
S5_HALF_IN = S5_HALF_GROUPS * S5_GROUP
S5_HALF_STATE = S5_HALF_GROUPS * S5_STATE
POOL_WINDOWS = (2, 4, 8, 16)
POOL_GROUP = WIDTH // len(POOL_WINDOWS)
POOL_HALO = 16
SCONV_WIDTH = 3
CONV_HALO = 8
MLSTM_HEADS = 4
MLSTM_HEAD_DIM = WIDTH // MLSTM_HEADS
MLSTM_CONV_WIDTH = 4
MLSTM_CHUNK = 256
LANES = 128
GATE_LANES = LANES

IN_SIZES = (WIDTH, WIDTH, WIDTH, WIDTH, WIDTH, WIDTH, WIDTH, WIDTH,
            2 * WIDTH, WIDTH, WIDTH, MLSTM_HEADS, MLSTM_HEADS, WIDTH, N_BRANCHES * D_MODEL)
_OFF = np.concatenate([[0], np.cumsum(IN_SIZES)]).tolist()
(O_S5U, O_S5Z, O_PU, O_PZ, O_SCX, O_SCB, O_SCC, O_SCZ,
 O_QK, O_V, O_O, O_I, O_F, O_Z, O_GATE, O_END) = _OFF

VMEM_LIMIT_BYTES = 56 * 1024 * 1024


def _rms(x, w):
    ms = jnp.mean(x * x, axis=-1, keepdims=True)
    return x * lax.rsqrt(ms + RMS_EPS) * w


def _sigmoid(x):
    return 1.0 / (1.0 + jnp.exp(-x))


def _silu(x):
    return x * _sigmoid(x)


def _gelu_tanh(x):
    c = np.float32(np.sqrt(2.0 / np.pi))
    return x * (0.5 * (1.0 + jnp.tanh(c * (x + 0.044715 * (x * x * x)))))


def _log_sigmoid(x):
    return jnp.minimum(x, 0.0) - jnp.log1p(jnp.exp(-jnp.abs(x)))


def _dot(a, b):
    return jnp.dot(a, b, preferred_element_type=F32)


def _const_spec(shape):
    nd = len(shape)
    return pl.BlockSpec(shape, lambda *_: (0,) * nd, pipeline_mode=pl.Buffered(1))


def _s5_kernel(x_ref, npre_ref, wuz_ref, wg_ref, bblk_ref, cblk_ref, are_ref, aim_ref,
               dskip_ref, wglu_ref, wb_ref, out_ref, bu_ref, st_ref, gate_ref, us_ref, ys_ref,
               *, nb, tt, pitch, pitch_t, scan_lanes):
    r = nb * tt

    @pl.when(pl.program_id(0) == 0)
    def _():
        st_ref[...] = jnp.zeros_like(st_ref)

    x = x_ref[...].reshape(r, D_MODEL)
    hb = _rms(x, npre_ref[...]).astype(BF16)
    uz = _dot(hb, wuz_ref[...])
    u = uz[:, :WIDTH]
    z = uz[:, WIDTH:]

    n_u = WIDTH // LANES
    for b in range(nb):
        for k in range(n_u):
            us_ref[k, b * pitch:b * pitch + tt, :] = u[b * tt:(b + 1) * tt, k * LANES:(k + 1) * LANES]
    u_tb = jnp.concatenate(
        [jnp.concatenate([us_ref[k, pl.ds(t, nb, stride=pitch), :] for k in range(n_u)], axis=1)
         for t in range(tt)], axis=0)
    ub = u_tb.astype(BF16)

    def b_stage(half):
        cols = slice(half * 2 * S5_HALF_STATE, (half + 1) * 2 * S5_HALF_STATE)
        bu_ref[:, cols] = _dot(ub[:, half * S5_HALF_IN:(half + 1) * S5_HALF_IN], bblk_ref[half])

    def scan_stage(half):
        for c0 in range(0, S5_HALF_STATE, scan_lanes):
            re0 = half * 2 * S5_HALF_STATE + c0
            im0 = re0 + S5_HALF_STATE
            s0 = half * S5_HALF_STATE + c0
            ar = jnp.broadcast_to(are_ref[:, s0:s0 + scan_lanes], (nb, scan_lanes))
            ai = jnp.broadcast_to(aim_ref[:, s0:s0 + scan_lanes], (nb, scan_lanes))
            xr = st_ref[:, re0:re0 + scan_lanes]
            xi = st_ref[:, im0:im0 + scan_lanes]
            for t in range(tt):
                rows = slice(t * nb, (t + 1) * nb)
                nr = ar * xr - ai * xi + bu_ref[rows, re0:re0 + scan_lanes]
                ni = ar * xi + ai * xr + bu_ref[rows, im0:im0 + scan_lanes]
                bu_ref[rows, re0:re0 + scan_lanes] = nr
                bu_ref[rows, im0:im0 + scan_lanes] = ni
                xr, xi = nr, ni
            st_ref[:, re0:re0 + scan_lanes] = xr
            st_ref[:, im0:im0 + scan_lanes] = xi

    def c_stage(half):
        cols = slice(half * 2 * S5_HALF_STATE, (half + 1) * 2 * S5_HALF_STATE)
        return _dot(bu_ref[:, cols].astype(BF16), cblk_ref[half])

    b_stage(0)
    b_stage(1)
    gate_ref[...] = _dot(hb, wg_ref[...])
    scan_stage(0)
    y0 = c_stage(0)
    scan_stage(1)
    y1 = c_stage(1)
    y_tb = jnp.concatenate([y0, y1], axis=1) + dskip_ref[...] * u_tb

    for t in range(tt):
        for k in range(n_u):
            ys_ref[k, t * pitch_t:t * pitch_t + nb, :] = y_tb[t * nb:(t + 1) * nb, k * LANES:(k + 1) * LANES]
    y = jnp.concatenate(
        [jnp.concatenate([ys_ref[k, pl.ds(b, tt, stride=pitch_t), :] for k in range(n_u)], axis=1)
         for b in range(nb)], axis=0)
    y = _gelu_tanh(y)
    y = y * _sigmoid(_dot(y.astype(BF16), wglu_ref[...]))
    pb = _dot((y * _silu(z)).astype(BF16), wb_ref[...])
    out_ref[...] = (_sigmoid(gate_ref[...]) * pb).reshape(nb, tt, D_MODEL)


def _odd_tile_pitch(n):
    return n if (n // 8) % 2 == 1 else n + 8


def _s5_call(x, npre, wuz, wg, bblk, cblk, are, aim, dskip, wglu, wb, *, tt, scan_lanes=512):
    nb, s, _ = x.shape
    assert s % tt == 0 and tt % 8 == 0 and nb % 8 == 0
    pitch, pitch_t = _odd_tile_pitch(tt), _odd_tile_pitch(nb)
    kern = functools.partial(_s5_kernel, nb=nb, tt=tt, pitch=pitch, pitch_t=pitch_t,
                             scan_lanes=scan_lanes)
    row_spec = pl.BlockSpec((nb, tt, D_MODEL), lambda i: (0, i, 0))
    consts = (npre, wuz, wg, bblk, cblk, are, aim, dskip, wglu, wb)
    return pl.pallas_call(
        kern,
        out_shape=jax.ShapeDtypeStruct(x.shape, F32),
        grid=(s // tt,),
        in_specs=[row_spec] + [_const_spec(c.shape) for c in consts],
        out_specs=row_spec,
        scratch_shapes=[pltpu.VMEM((nb * tt, 4 * S5_HALF_STATE), F32),
                        pltpu.VMEM((nb, 4 * S5_HALF_STATE), F32),
                        pltpu.VMEM((nb * tt, D_MODEL), F32),
                        pltpu.VMEM((WIDTH // LANES, nb * pitch, LANES), F32),
                        pltpu.VMEM((WIDTH // LANES, tt * pitch_t, LANES), F32)],
        compiler_params=pltpu.CompilerParams(
            dimension_semantics=("arbitrary",), vmem_limit_bytes=VMEM_LIMIT_BYTES),
        name="s5_branch",
    )(x, *consts)


def _pool_sconv_kernel(x_ref, acc_ref, npre_ref, wp_ref, wg_ref, poolw_ref, pscale_ref, scw_ref,
                       wb_ref, out_ref, pu_ref, cx_ref, gate_ref, *, tm):
    s_idx = pl.program_id(1)

    @pl.when(s_idx == 0)
    def _():
        pu_ref[0:POOL_HALO, :] = jnp.zeros((POOL_HALO, WIDTH), F32)
        cx_ref[0:CONV_HALO, :] = jnp.zeros((CONV_HALO, WIDTH), F32)

    @pl.when(s_idx > 0)
    def _():
        pu_ref[0:POOL_HALO, :] = pu_ref[tm:tm + POOL_HALO, :]
        cx_ref[0:CONV_HALO, :] = cx_ref[tm:tm + CONV_HALO, :]

    hb = _rms(x_ref[...], npre_ref[...]).astype(BF16)
    proj = _dot(hb, wp_ref[...])
    gate_ref[...] = _dot(hb, wg_ref[...])
    pool_u = proj[:, 0 * WIDTH:1 * WIDTH]
    pool_z = proj[:, 1 * WIDTH:2 * WIDTH]
    sc_x = proj[:, 2 * WIDTH:3 * WIDTH]
    sc_b = proj[:, 3 * WIDTH:4 * WIDTH]
    sc_c = proj[:, 4 * WIDTH:5 * WIDTH]
    sc_z = proj[:, 5 * WIDTH:6 * WIDTH]

    pu_ref[POOL_HALO:POOL_HALO + tm, :] = pool_u
    t_glob = s_idx * tm + lax.broadcasted_iota(jnp.int32, (tm, 1), 0)
    mixed = []
    for gi, win in enumerate(POOL_WINDOWS):
        lanes = slice(gi * POOL_GROUP, (gi + 1) * POOL_GROUP)
        wsum = pu_ref[POOL_HALO:POOL_HALO + tm, lanes]
        for j in range(1, win):
            wsum = wsum + pu_ref[POOL_HALO - j:POOL_HALO - j + tm, lanes]
        cnt = jnp.minimum(t_glob + 1, win).astype(F32)
        pooled = wsum / cnt - pool_u[:, lanes]
        mixed.append(_dot(pooled.astype(BF16), poolw_ref[gi]))
    y_b = jnp.concatenate(mixed, axis=1) * pscale_ref[...]

    cx = sc_c * sc_x
    cx_ref[CONV_HALO:CONV_HALO + tm, :] = cx
    conv = scw_ref[SCONV_WIDTH - 1:SCONV_WIDTH, :] * cx
    for k in range(SCONV_WIDTH - 1):
        sh = SCONV_WIDTH - 1 - k
        conv = conv + scw_ref[k:k + 1, :] * cx_ref[CONV_HALO - sh:CONV_HALO - sh + tm, :]
    y_c = sc_b * conv

    p_b = _dot((y_b * _silu(pool_z)).astype(BF16), wb_ref[0])
    p_c = _dot((y_c * _silu(sc_z)).astype(BF16), wb_ref[1])
    gates = _sigmoid(gate_ref[...])
    out_ref[...] = acc_ref[...] + gates[:, :D_MODEL] * p_b + gates[:, D_MODEL:] * p_c


def _pool_sconv_call(x, acc, npre, wp, wg, poolw, pscale, scw, wb, *, tm):
    bsz, s, _ = x.shape
    assert s % tm == 0 and tm % 8 == 0 and tm >= POOL_HALO
    kern = functools.partial(_pool_sconv_kernel, tm=tm)
    row_spec = pl.BlockSpec((None, tm, D_MODEL), lambda b, i: (b, i, 0))
    consts = (npre, wp, wg, poolw, pscale, scw, wb)
    return pl.pallas_call(
        kern,
        out_shape=jax.ShapeDtypeStruct(x.shape, F32),
        grid=(bsz, s // tm),
        in_specs=[row_spec, row_spec] + [_const_spec(c.shape) for c in consts],
        out_specs=row_spec,
        scratch_shapes=[pltpu.VMEM((POOL_HALO + tm, WIDTH), F32),
                        pltpu.VMEM((CONV_HALO + tm, WIDTH), F32),
                        pltpu.VMEM((tm, 2 * D_MODEL), F32)],
        compiler_params=pltpu.CompilerParams(
            dimension_semantics=("arbitrary", "arbitrary"), vmem_limit_bytes=VMEM_LIMIT_BYTES),
        name="pool_sconv_branch",
    )(x, acc, *consts)


def _mlstm_kernel(x_ref, acc_ref, npre_ref, npost_ref, wc_ref, woz_ref, wg_ref, convw_ref, bi_ref,
                  bf_ref, mnorm_ref, wb_ref, wout_ref, out_ref,
                  qk_ref, h_ref, c_ref, m_ref, oz_ref, gate_ref, *, tm):
    s_idx = pl.program_id(1)
    hh_n, dh, ll = MLSTM_HEADS, MLSTM_HEAD_DIM, MLSTM_CHUNK

    @pl.when(s_idx == 0)
    def _():
        qk_ref[0:CONV_HALO, :] = jnp.zeros((CONV_HALO, 2 * WIDTH), F32)
        c_ref[...] = jnp.zeros_like(c_ref)
        m_ref[...] = jnp.zeros_like(m_ref)

    @pl.when(s_idx > 0)
    def _():
        qk_ref[0:CONV_HALO, :] = qk_ref[tm:tm + CONV_HALO, :]

    x = x_ref[...]
    hb = _rms(x, npre_ref[...]).astype(BF16)
    proj = _dot(hb, wc_ref[...])
    oz_ref[...] = _dot(hb, woz_ref[...])
    v_all = proj[:, 2 * WIDTH:3 * WIDTH]
    i_log = proj[:, 3 * WIDTH:3 * WIDTH + GATE_LANES] + bi_ref[...]
    log_f = _log_sigmoid(proj[:, 3 * WIDTH + GATE_LANES:3 * WIDTH + 2 * GATE_LANES] + bf_ref[...])

    qk_ref[CONV_HALO:CONV_HALO + tm, :] = proj[:, :2 * WIDTH]
    conv = convw_ref[MLSTM_CONV_WIDTH - 1:MLSTM_CONV_WIDTH, :] * proj[:, :2 * WIDTH]
    for k in range(MLSTM_CONV_WIDTH - 1):
        sh = MLSTM_CONV_WIDTH - 1 - k
        conv = conv + convw_ref[k:k + 1, :] * qk_ref[CONV_HALO - sh:CONV_HALO - sh + tm, :]
    qk_act = _silu(conv)
    q_all = qk_act[:, :WIDTH] * np.float32(dh ** -0.5)
    k_all = qk_act[:, WIDTH:]

    row_i = lax.broadcasted_iota(jnp.int32, (ll, ll), 0)
    col_i = lax.broadcasted_iota(jnp.int32, (ll, ll), 1)
    causal = col_i <= row_i
    tri = causal.astype(BF16)

    ones_blk = (lax.broadcasted_iota(jnp.int32, (ll, dh), 1) == 0).astype(BF16)

    blocks = [(c, hh) for c in range(tm // ll) for hh in range(hh_n)]
    rows_of = lambda c: slice(c * ll, (c + 1) * ll)
    lanes_of = lambda hh: slice(hh * dh, (hh + 1) * dh)
    b_cs, s_qk, qhb = {}, {}, {}
    for c in range(tm // ll):
        lf_c = log_f[rows_of(c)]
        lf_hi = lf_c.astype(BF16)
        lf_lo = (lf_c - lf_hi.astype(F32)).astype(BF16)
        b_cs[c] = _dot(tri, lf_hi) + _dot(tri, lf_lo)
    for c, hh in blocks:
        qhb[c, hh] = q_all[rows_of(c), lanes_of(hh)].astype(BF16)
        s_qk[c, hh] = lax.dot_general(qhb[c, hh], k_all[rows_of(c), lanes_of(hh)].astype(BF16),
                                      (((1,), (1,)), ((), ())), preferred_element_type=F32)
    gate_ref[...] = _dot(hb, wg_ref[...])

    stage_b = {}
    for c in range(tm // ll):
        r_col = i_log[rows_of(c)] - b_cs[c]
        r_row = r_col.T
        g_tot = b_cs[c][ll - 1:ll, :]
        a_col = g_tot + r_col
        for hh in range(hh_n):
            b_h = b_cs[c][:, hh:hh + 1]
            d_log = jnp.where(causal, b_h + r_row[hh:hh + 1, :], -jnp.inf)
            m_in = jnp.max(d_log, axis=1, keepdims=True)
            w_un = (jnp.exp(d_log - m_in) * s_qk[c, hh]).astype(BF16)
            a_h = a_col[:, hh:hh + 1]
            m_loc = jnp.max(a_h, axis=0, keepdims=True)
            kw = (jnp.exp(a_h - m_loc) * k_all[rows_of(c), lanes_of(hh)]).astype(BF16)
            stage_b[c, hh] = (b_h, m_in, w_un, m_loc, kw, g_tot[:, hh:hh + 1])

    pre = []
    for c, hh in blocks:
        b_h, m_in, w_un, m_loc, kw, g_h = stage_b[c, hh]
        vext = jnp.concatenate([v_all[rows_of(c), lanes_of(hh)].astype(BF16), ones_blk], axis=1)
        nd_in = _dot(w_un, vext)
        cn_loc = lax.dot_general(kw, vext, (((0,), (0,)), ((), ())),
                                 preferred_element_type=F32)
        pre.append((rows_of(c), lanes_of(hh), hh, qhb[c, hh], b_h, m_in, nd_in, m_loc, cn_loc, g_h))

    for rows, lanes, hh, qhb, b_h, m_in, nd_in, m_loc, cn_loc, g_h in pre:
        m_prev = m_ref[hh][:, 0:1]
        cn_prev = c_ref[hh]
        e_log = b_h + m_prev
        m_t = jnp.maximum(m_in, e_log)
        tot = (jnp.exp(m_in - m_t) * nd_in
               + jnp.exp(e_log - m_t) * _dot(qhb, cn_prev.astype(BF16)))
        h_ref[rows, lanes] = tot[:, :dh] / jnp.maximum(jnp.abs(tot[:, dh:dh + 1]), jnp.exp(-m_t))
        m_new = jnp.maximum(g_h + m_prev, m_loc)
        c_ref[hh] = jnp.exp(g_h + m_prev - m_new) * cn_prev + jnp.exp(m_loc - m_new) * cn_loc
        m_ref[hh] = jnp.broadcast_to(m_new, (1, dh))

    hg = h_ref[...] * _sigmoid(oz_ref[:, :WIDTH])
    normed = []
    for hh in range(hh_n):
        blk = hg[:, hh * dh:(hh + 1) * dh]
        normed.append(blk * lax.rsqrt(jnp.mean(blk * blk, axis=-1, keepdims=True) + RMS_EPS))
    y_d = jnp.concatenate(normed, axis=1) * mnorm_ref[...]

    p_d = _dot((y_d * _silu(oz_ref[:, WIDTH:])).astype(BF16), wb_ref[...])
    gate = _sigmoid(gate_ref[...])
    merged = acc_ref[...] + gate * p_d
    out = _dot(merged.astype(BF16), wout_ref[...])
    out_ref[...] = x + _rms(out, npost_ref[...])


def _mlstm_call(x, acc, npre, npost, wc, woz, wg, convw, b_i, b_f, mnorm, wb, wout, *, tm):
    bsz, s, _ = x.shape
    assert s % tm == 0 and tm % MLSTM_CHUNK == 0
    kern = functools.partial(_mlstm_kernel, tm=tm)
    row_spec = pl.BlockSpec((None, tm, D_MODEL), lambda b, i: (b, i, 0))
    consts = (npre, npost, wc, woz, wg, convw, b_i, b_f, mnorm, wb, wout)
    return pl.pallas_call(
        kern,
        out_shape=jax.ShapeDtypeStruct(x.shape, F32),
        grid=(bsz, s // tm),
        in_specs=[row_spec, row_spec] + [_const_spec(c.shape) for c in consts],
        out_specs=row_spec,
        scratch_shapes=[pltpu.VMEM((CONV_HALO + tm, 2 * WIDTH), F32),
                        pltpu.VMEM((tm, WIDTH), F32),
                        pltpu.VMEM((MLSTM_HEADS, MLSTM_HEAD_DIM, 2 * MLSTM_HEAD_DIM), F32),
                        pltpu.VMEM((MLSTM_HEADS, 1, MLSTM_HEAD_DIM), F32),
                        pltpu.VMEM((tm, 2 * WIDTH), F32),
                        pltpu.VMEM((tm, D_MODEL), F32)],
        compiler_params=pltpu.CompilerParams(
            dimension_semantics=("arbitrary", "arbitrary"), vmem_limit_bytes=VMEM_LIMIT_BYTES),
        name="mlstm_merge_out",
    )(x, acc, *consts)


def _s5_params(a_re, a_im, log_dt, b_re, b_im, c_re, c_im):
    lam = lax.complex(a_re.astype(F32), a_im.astype(F32))
    dt = jnp.exp(log_dt.astype(F32))[:, None]
    a_bar = jnp.exp(lam * dt)
    b_bar = ((a_bar - 1.0) / lam)[..., None] * lax.complex(b_re.astype(F32), b_im.astype(F32))
    eye = jnp.eye(S5_HALF_GROUPS, dtype=F32)

    def block_in(m):
        return jnp.einsum('gnp,gh->gphn', m, eye).reshape(S5_HALF_IN, S5_HALF_STATE)

    def block_out(m):
        return jnp.einsum('gpn,gh->gnhp', m, eye).reshape(S5_HALF_STATE, S5_HALF_IN)

    bblk, cblk = [], []
    for half in range(2):
        gs = slice(half * S5_HALF_GROUPS, (half + 1) * S5_HALF_GROUPS)
        bblk.append(jnp.concatenate([block_in(jnp.real(b_bar[gs])), block_in(jnp.imag(b_bar[gs]))], axis=1))
        cblk.append(jnp.concatenate([block_out(c_re[gs].astype(F32)), block_out(-c_im[gs].astype(F32))], axis=0))
    bblk = jnp.stack(bblk).astype(BF16)
    cblk = jnp.stack(cblk).astype(BF16)
    are = jnp.real(a_bar).reshape(1, S5_GROUPS * S5_STATE)
    aim = jnp.imag(a_bar).reshape(1, S5_GROUPS * S5_STATE)
    return bblk, cblk, are, aim


def _pad_lanes(v, n):
    return jnp.pad(v.astype(F32), (0, n - v.shape[0])).reshape(1, n)


def _pad_rows(w, n):
    return jnp.pad(w.astype(F32), ((0, n - w.shape[0]), (0, 0)))


def _forward(x, norm_pre_w, norm_post_w, w_in, s5_A_re, s5_A_im, s5_log_dt, s5_B_re, s5_B_im,
             s5_C_re, s5_C_im, s5_D, s5_w_glu, pool_w, pool_scale, sconv_w, mlstm_conv_w,
             mlstm_b_i, mlstm_b_f, mlstm_norm_w, w_branch, w_out, *, tt, tm_ps, tm_ml):
    depth = w_in.shape[0]
    row = lambda v: v.astype(F32).reshape(1, -1)
    for l in range(depth):
        wl = w_in[l]
        gate_w = lambda bi: wl[:, O_GATE + bi * D_MODEL:O_GATE + (bi + 1) * D_MODEL]
        npre = row(norm_pre_w[l])
        bblk, cblk, are, aim = _s5_params(s5_A_re[l], s5_A_im[l], s5_log_dt[l], s5_B_re[l],
                                          s5_B_im[l], s5_C_re[l], s5_C_im[l])
        acc = _s5_call(
            x, npre, wl[:, O_S5U:O_PU].astype(BF16), gate_w(0).astype(BF16),
            bblk, cblk, are, aim, row(s5_D[l]), s5_w_glu[l].astype(BF16),
            w_branch[l, 0].astype(BF16), tt=tt)
        acc = _pool_sconv_call(
            x, acc, npre, wl[:, O_PU:O_QK].astype(BF16),
            jnp.concatenate([gate_w(1), gate_w(2)], axis=1).astype(BF16),
            pool_w[l].astype(BF16), row(pool_scale[l]), _pad_rows(sconv_w[l], 8),
            w_branch[l, 1:3].astype(BF16), tm=tm_ps)
        zero_gate = jnp.zeros((D_MODEL, GATE_LANES - MLSTM_HEADS), w_in.dtype)
        wc = jnp.concatenate([wl[:, O_QK:O_O], wl[:, O_I:O_F], zero_gate,
                              wl[:, O_F:O_Z], zero_gate], axis=1).astype(BF16)
        woz = jnp.concatenate([wl[:, O_O:O_I], wl[:, O_Z:O_GATE]], axis=1).astype(BF16)
        x = _mlstm_call(
            x, acc, npre, row(norm_post_w[l]), wc, woz, gate_w(3).astype(BF16),
            _pad_rows(mlstm_conv_w[l], 8), _pad_lanes(mlstm_b_i[l], GATE_LANES),
            _pad_lanes(mlstm_b_f[l], GATE_LANES), row(mlstm_norm_w[l]),
            w_branch[l, 3].astype(BF16), w_out[l].astype(BF16), tm=tm_ml)
    return x


def kernel(x, norm_pre_w, norm_post_w, w_in, s5_A_re, s5_A_im, s5_log_dt, s5_B_re, s5_B_im, s5_C_re, s5_C_im, s5_D, s5_w_glu, pool_w, pool_scale, sconv_w, mlstm_conv_w, mlstm_b_i, mlstm_b_f, mlstm_norm_w, w_branch, w_out):
    return _forward(x, norm_pre_w, norm_post_w, w_in, s5_A_re, s5_A_im, s5_log_dt, s5_B_re,
                    s5_B_im, s5_C_re, s5_C_im, s5_D, s5_w_glu, pool_w, pool_scale, sconv_w,
                    mlstm_conv_w, mlstm_b_i, mlstm_b_f, mlstm_norm_w, w_branch, w_out,
                    tt=32, tm_ps=512, tm_ml=512)
```

```python
import functools

import jax
import jax.numpy as jnp
import numpy as np
from jax import lax
from jax.experimental import pallas as pl
from jax.experimental.pallas import tpu as pltpu

F32 = jnp.float32
BF16 = jnp.bfloat16

D_MODEL = 1024
N_BRANCHES = 4
WIDTH = 512
RMS_EPS = 1e-6
S5_GROUP = 16
S5_GROUPS = WIDTH // S5_GROUP
S5_STATE = 64
S5_HALF_GROUPS = 16
S5_HALF_IN = S5_HALF_GROUPS * S5_GROUP
S5_HALF_STATE = S5_HALF_GROUPS * S5_STATE
POOL_WINDOWS = (2, 4, 8, 16)
POOL_GROUP = WIDTH // len(POOL_WINDOWS)
POOL_HALO = 16
SCONV_WIDTH = 3
CONV_HALO = 8
MLSTM_HEADS = 4
MLSTM_HEAD_DIM = WIDTH // MLSTM_HEADS
MLSTM_CONV_WIDTH = 4
MLSTM_CHUNK = 256
LANES = 128
GATE_LANES = LANES

IN_SIZES = (WIDTH, WIDTH, WIDTH, WIDTH, WIDTH, WIDTH, WIDTH, WIDTH,
            2 * WIDTH, WIDTH, WIDTH, MLSTM_HEADS, MLSTM_HEADS, WIDTH, N_BRANCHES * D_MODEL)
_OFF = np.concatenate([[0], np.cumsum(IN_SIZES)]).tolist()
(O_S5U, O_S5Z, O_PU, O_PZ, O_SCX, O_SCB, O_SCC, O_SCZ,
 O_QK, O_V, O_O, O_I, O_F, O_Z, O_GATE, O_END) = _OFF

VMEM_LIMIT_BYTES = 56 * 1024 * 1024


def _rms(x, w):
    ms = jnp.mean(x * x, axis=-1, keepdims=True)
    return x * lax.rsqrt(ms + RMS_EPS) * w


def _sigmoid(x):
    return 0.5 * jnp.tanh(0.5 * x) + 0.5


def _silu(x):
    return x * _sigmoid(x)


def _gelu_tanh(x):
    c = np.float32(np.sqrt(2.0 / np.pi))
    return x * (0.5 * (1.0 + jnp.tanh(c * (x + 0.044715 * (x * x * x)))))


def _log_sigmoid(x):
    return jnp.minimum(x, 0.0) - jnp.log1p(jnp.exp(-jnp.abs(x)))


def _dot(a, b):
    return jnp.dot(a, b, preferred_element_type=F32)


def _const_spec(shape):
    nd = len(shape)
    return pl.BlockSpec(shape, lambda *_: (0,) * nd, pipeline_mode=pl.Buffered(1))


def _s5_kernel(x_ref, npre_ref, wuz_ref, wg_ref, bblk_ref, cblk_ref, are_ref, aim_ref,
               dskip_ref, wglu_ref, wb_ref, out_ref, hb_out_ref, bu_ref, st_ref, gate_ref, us_ref,
               ys_ref,
               *, nb, tt, pitch, pitch_t, scan_lanes):
    r = nb * tt

    @pl.when(pl.program_id(0) == 0)
    def _():
        st_ref[...] = jnp.zeros_like(st_ref)

    x = x_ref[...].reshape(r, D_MODEL)
    hb = _rms(x, npre_ref[...]).astype(BF16)
    hb_out_ref[...] = hb.reshape(nb, tt, D_MODEL)
    uz = _dot(hb, wuz_ref[...])
    u = uz[:, :WIDTH]
    z = uz[:, WIDTH:]

    n_u = WIDTH // LANES
    for b in range(nb):
        for k in range(n_u):
            us_ref[k, b * pitch:b * pitch + tt, :] = u[b * tt:(b + 1) * tt, k * LANES:(k + 1) * LANES]
    u_tb = jnp.concatenate(
        [jnp.concatenate([us_ref[k, pl.ds(t, nb, stride=pitch), :] for k in range(n_u)], axis=1)
         for t in range(tt)], axis=0)
    ub = u_tb.astype(BF16)

    def b_stage(half):
        cols = slice(half * 2 * S5_HALF_STATE, (half + 1) * 2 * S5_HALF_STATE)
        bu_ref[:, cols] = _dot(ub[:, half * S5_HALF_IN:(half + 1) * S5_HALF_IN], bblk_ref[half])

    def scan_stage(half):
        for c0 in range(0, S5_HALF_STATE, scan_lanes):
            re0 = half * 2 * S5_HALF_STATE + c0
            im0 = re0 + S5_HALF_STATE
            s0 = half * S5_HALF_STATE + c0
            ar = jnp.broadcast_to(are_ref[:, s0:s0 + scan_lanes], (nb, scan_lanes))
            ai = jnp.broadcast_to(aim_ref[:, s0:s0 + scan_lanes], (nb, scan_lanes))
            xr = st_ref[:, re0:re0 + scan_lanes]
            xi = st_ref[:, im0:im0 + scan_lanes]
            for t in range(tt):
                rows = slice(t * nb, (t + 1) * nb)
                nr = ar * xr - ai * xi + bu_ref[rows, re0:re0 + scan_lanes]
                ni = ar * xi + ai * xr + bu_ref[rows, im0:im0 + scan_lanes]
                bu_ref[rows, re0:re0 + scan_lanes] = nr
                bu_ref[rows, im0:im0 + scan_lanes] = ni
                xr, xi = nr, ni
            st_ref[:, re0:re0 + scan_lanes] = xr
            st_ref[:, im0:im0 + scan_lanes] = xi

    def c_stage(half):
        cols = slice(half * 2 * S5_HALF_STATE, (half + 1) * 2 * S5_HALF_STATE)
        return _dot(bu_ref[:, cols].astype(BF16), cblk_ref[half])

    b_stage(0)
    b_stage(1)
    gate_ref[...] = _sigmoid(_dot(hb, wg_ref[...]))
    scan_stage(0)
    y0 = c_stage(0)
    scan_stage(1)
    y1 = c_stage(1)
    y_tb = jnp.concatenate([y0, y1], axis=1) + dskip_ref[...] * u_tb

    for t in range(tt):
        for k in range(n_u):
            ys_ref[k, t * pitch_t:t * pitch_t + nb, :] = y_tb[t * nb:(t + 1) * nb, k * LANES:(k + 1) * LANES]
    y = jnp.concatenate(
        [jnp.concatenate([ys_ref[k, pl.ds(b, tt, stride=pitch_t), :] for k in range(n_u)], axis=1)
         for b in range(nb)], axis=0)
    y = _gelu_tanh(y)
    y = y * _sigmoid(_dot(y.astype(BF16), wglu_ref[...]))
    pb = _dot((y * _silu(z)).astype(BF16), wb_ref[...])
    out_ref[...] = (gate_ref[...] * pb).reshape(nb, tt, D_MODEL)


def _odd_tile_pitch(n):
    return n if (n // 8) % 2 == 1 else n + 8


def _s5_call(x, npre, wuz, wg, bblk, cblk, are, aim, dskip, wglu, wb, *, tt, scan_lanes=512):
    nb, s, _ = x.shape
    assert s % tt == 0 and tt % 8 == 0 and nb % 8 == 0
    pitch, pitch_t = _odd_tile_pitch(tt), _odd_tile_pitch(nb)
    kern = functools.partial(_s5_kernel, nb=nb, tt=tt, pitch=pitch, pitch_t=pitch_t,
                             scan_lanes=scan_lanes)
    row_spec = pl.BlockSpec((nb, tt, D_MODEL), lambda i: (0, i, 0))
    consts = (npre, wuz, wg, bblk, cblk, are, aim, dskip, wglu, wb)
    return pl.pallas_call(
        kern,
        out_shape=(jax.ShapeDtypeStruct(x.shape, F32), jax.ShapeDtypeStruct(x.shape, BF16)),
        grid=(s // tt,),
        in_specs=[row_spec] + [_const_spec(c.shape) for c in consts],
        out_specs=(row_spec, row_spec),
        scratch_shapes=[pltpu.VMEM((nb * tt, 4 * S5_HALF_STATE), F32),
                        pltpu.VMEM((nb, 4 * S5_HALF_STATE), F32),
                        pltpu.VMEM((nb * tt, D_MODEL), F32),
                        pltpu.VMEM((WIDTH // LANES, nb * pitch, LANES), F32),
                        pltpu.VMEM((WIDTH // LANES, tt * pitch_t, LANES), F32)],
        compiler_params=pltpu.CompilerParams(
            dimension_semantics=("arbitrary",), vmem_limit_bytes=VMEM_LIMIT_BYTES),
        name="s5_branch",
    )(x, *consts)


def _pool_sconv_kernel(hb_ref, acc_ref, wp_ref, wg_ref, poolw_ref, pscale_ref, scw_ref,
                       wb_ref, out_ref, pu_ref, cx_ref, gate_ref, *, tm):
    s_idx = pl.program_id(1)

    @pl.when(s_idx == 0)
    def _():
        pu_ref[0:POOL_HALO, :] = jnp.zeros((POOL_HALO, WIDTH), F32)
        cx_ref[0:CONV_HALO, :] = jnp.zeros((CONV_HALO, WIDTH), F32)

    @pl.when(s_idx > 0)
    def _():
        pu_ref[0:POOL_HALO, :] = pu_ref[tm:tm + POOL_HALO, :]
        cx_ref[0:CONV_HALO, :] = cx_ref[tm:tm + CONV_HALO, :]

    hb = hb_ref[...]
    proj = _dot(hb, wp_ref[...])
    gate_ref[...] = _sigmoid(_dot(hb, wg_ref[...]))
    pool_u = proj[:, 0 * WIDTH:1 * WIDTH]
    pool_z = proj[:, 1 * WIDTH:2 * WIDTH]
    sc_x = proj[:, 2 * WIDTH:3 * WIDTH]
    sc_b = proj[:, 3 * WIDTH:4 * WIDTH]
    sc_c = proj[:, 4 * WIDTH:5 * WIDTH]
    sc_z = proj[:, 5 * WIDTH:6 * WIDTH]

    pu_ref[POOL_HALO:POOL_HALO + tm, :] = pool_u
    t_glob = s_idx * tm + lax.broadcasted_iota(jnp.int32, (tm, 1), 0)
    mixed = []
    for gi, win in enumerate(POOL_WINDOWS):
        lanes = slice(gi * POOL_GROUP, (gi + 1) * POOL_GROUP)
        wsum = pu_ref[POOL_HALO:POOL_HALO + tm, lanes]
        for j in range(1, win):
            wsum = wsum + pu_ref[POOL_HALO - j:POOL_HALO - j + tm, lanes]
        cnt = jnp.minimum(t_glob + 1, win).astype(F32)
        pooled = wsum / cnt - pool_u[:, lanes]
        mixed.append(_dot(pooled.astype(BF16), poolw_ref[gi]))
    y_b = jnp.concatenate(mixed, axis=1) * pscale_ref[...]

    cx = sc_c * sc_x
    cx_ref[CONV_HALO:CONV_HALO + tm, :] = cx
    conv = scw_ref[SCONV_WIDTH - 1:SCONV_WIDTH, :] * cx
    for k in range(SCONV_WIDTH - 1):
        sh = SCONV_WIDTH - 1 - k
        conv = conv + scw_ref[k:k + 1, :] * cx_ref[CONV_HALO - sh:CONV_HALO - sh + tm, :]
    y_c = sc_b * conv

    p_b = _dot((y_b * _silu(pool_z)).astype(BF16), wb_ref[0])
    p_c = _dot((y_c * _silu(sc_z)).astype(BF16), wb_ref[1])
    gates = gate_ref[...]
    out_ref[...] = acc_ref[...] + gates[:, :D_MODEL] * p_b + gates[:, D_MODEL:] * p_c


def _pool_sconv_call(hb, acc, wp, wg, poolw, pscale, scw, wb, *, tm):
    bsz, s, _ = acc.shape
    assert s % tm == 0 and tm % 8 == 0 and tm >= POOL_HALO
    kern = functools.partial(_pool_sconv_kernel, tm=tm)
    row_spec = pl.BlockSpec((None, tm, D_MODEL), lambda b, i: (b, i, 0))
    consts = (wp, wg, poolw, pscale, scw, wb)
    return pl.pallas_call(
        kern,
        out_shape=jax.ShapeDtypeStruct(acc.shape, F32),
        grid=(bsz, s // tm),
        in_specs=[row_spec, row_spec] + [_const_spec(c.shape) for c in consts],
        out_specs=row_spec,
        scratch_shapes=[pltpu.VMEM((POOL_HALO + tm, WIDTH), F32),
                        pltpu.VMEM((CONV_HALO + tm, WIDTH), F32),
                        pltpu.VMEM((tm, 2 * D_MODEL), F32)],
        compiler_params=pltpu.CompilerParams(
            dimension_semantics=("arbitrary", "arbitrary"), vmem_limit_bytes=VMEM_LIMIT_BYTES),
        name="pool_sconv_branch",
    )(hb, acc, *consts)


def _mlstm_kernel(x_ref, hb_ref, acc_ref, npost_ref, wc_ref, woz_ref, wg_ref, convw_ref, bi_ref,
                  bf_ref, mnorm_ref, wb_ref, wout_ref, out_ref,
                  qk_ref, h_ref, c_ref, m_ref, oz_ref, gate_ref, *, tm):
    s_idx = pl.program_id(1)
    hh_n, dh, ll = MLSTM_HEADS, MLSTM_HEAD_DIM, MLSTM_CHUNK

    @pl.when(s_idx == 0)
    def _():
        qk_ref[0:CONV_HALO, :] = jnp.zeros((CONV_HALO, 2 * WIDTH), F32)
        c_ref[...] = jnp.zeros_like(c_ref)
        m_ref[...] = jnp.zeros_like(m_ref)

    @pl.when(s_idx > 0)
    def _():
        qk_ref[0:CONV_HALO, :] = qk_ref[tm:tm + CONV_HALO, :]

    hb = hb_ref[...]
    proj = _dot(hb, wc_ref[...])
    oz_ref[...] = _dot(hb, woz_ref[...])
    v_all = proj[:, 2 * WIDTH:3 * WIDTH]
    i_log = proj[:, 3 * WIDTH:3 * WIDTH + GATE_LANES] + bi_ref[...]
    log_f = _log_sigmoid(proj[:, 3 * WIDTH + GATE_LANES:3 * WIDTH + 2 * GATE_LANES] + bf_ref[...])

    qk_ref[CONV_HALO:CONV_HALO + tm, :] = proj[:, :2 * WIDTH]
    conv = convw_ref[MLSTM_CONV_WIDTH - 1:MLSTM_CONV_WIDTH, :] * proj[:, :2 * WIDTH]
    for k in range(MLSTM_CONV_WIDTH - 1):
        sh = MLSTM_CONV_WIDTH - 1 - k
        conv = conv + convw_ref[k:k + 1, :] * qk_ref[CONV_HALO - sh:CONV_HALO - sh + tm, :]
    qk_act = _silu(conv)
    q_all = qk_act[:, :WIDTH] * np.float32(dh ** -0.5)
    k_all = qk_act[:, WIDTH:]

    row_i = lax.broadcasted_iota(jnp.int32, (ll, ll), 0)
    col_i = lax.broadcasted_iota(jnp.int32, (ll, ll), 1)
    causal = col_i <= row_i
    tri = causal.astype(BF16)

    ones_blk = (lax.broadcasted_iota(jnp.int32, (ll, dh), 1) == 0).astype(BF16)

    blocks = [(c, hh) for c in range(tm // ll) for hh in range(hh_n)]
    rows_of = lambda c: slice(c * ll, (c + 1) * ll)
    lanes_of = lambda hh: slice(hh * dh, (hh + 1) * dh)
    b_cs, s_qk, qhb = {}, {}, {}
    for c in range(tm // ll):
        lf_c = log_f[rows_of(c)]
        lf_hi = lf_c.astype(BF16)
        lf_lo = (lf_c - lf_hi.astype(F32)).astype(BF16)
        b_cs[c] = _dot(tri, lf_hi) + _dot(tri, lf_lo)
    for c, hh in blocks:
        qhb[c, hh] = q_all[rows_of(c), lanes_of(hh)].astype(BF16)
        s_qk[c, hh] = lax.dot_general(qhb[c, hh], k_all[rows_of(c), lanes_of(hh)].astype(BF16),
                                      (((1,), (1,)), ((), ())), preferred_element_type=F32)
    gate_ref[...] = _sigmoid(_dot(hb, wg_ref[...]))

    stage_b = {}
    for c in range(tm // ll):
        r_col = i_log[rows_of(c)] - b_cs[c]
        r_row = r_col.T
        g_tot = b_cs[c][ll - 1:ll, :]
        a_col = g_tot + r_col
        for hh in range(hh_n):
            b_h = b_cs[c][:, hh:hh + 1]
            r_msk = jnp.where(causal, r_row[hh:hh + 1, :], -jnp.inf)
            r_max = jnp.max(r_msk, axis=1, keepdims=True)
            m_in = b_h + r_max
            w_un = (jnp.exp(r_msk - r_max) * s_qk[c, hh]).astype(BF16)
            a_h = a_col[:, hh:hh + 1]
            m_loc = jnp.max(a_h, axis=0, keepdims=True)
            kw = (jnp.exp(a_h - m_loc) * k_all[rows_of(c), lanes_of(hh)]).astype(BF16)
            stage_b[c, hh] = (b_h, m_in, w_un, m_loc, kw, g_tot[:, hh:hh + 1])

    nd_in, cn_loc = {}, {}
    for c, hh in blocks:
        _, _, w_un, _, kw, _ = stage_b[c, hh]
        vext = jnp.concatenate([v_all[rows_of(c), lanes_of(hh)].astype(BF16), ones_blk], axis=1)
        nd_in[c, hh] = _dot(w_un, vext)
        cn_loc[c, hh] = lax.dot_general(kw, vext, (((0,), (0,)), ((), ())),
                                        preferred_element_type=F32)

    def finish(rows):
        hg = h_ref[rows, :] * _sigmoid(oz_ref[rows, :WIDTH])
        normed = []
        for hh in range(hh_n):
            blk = hg[:, hh * dh:(hh + 1) * dh]
            normed.append(blk * lax.rsqrt(jnp.mean(blk * blk, axis=-1, keepdims=True) + RMS_EPS))
        y_d = jnp.concatenate(normed, axis=1) * mnorm_ref[...]
        p_d = _dot((y_d * _silu(oz_ref[rows, WIDTH:])).astype(BF16), wb_ref[...])
        merged = acc_ref[rows, :] + gate_ref[rows, :] * p_d
        out = _dot(merged.astype(BF16), wout_ref[...])
        out_ref[rows, :] = x_ref[rows, :] + _rms(out, npost_ref[...])

    state = [(c_ref[hh], m_ref[hh][:, 0:1]) for hh in range(hh_n)]
    for c in range(tm // ll):
        inter = [_dot(qhb[c, hh], state[hh][0].astype(BF16)) for hh in range(hh_n)]
        for hh in range(hh_n):
            b_h, m_in, _, m_loc, _, g_h = stage_b[c, hh]
            cn_prev, m_prev = state[hh]
            e_log = b_h + m_prev
            m_t = jnp.maximum(m_in, e_log)
            tot = jnp.exp(m_in - m_t) * nd_in[c, hh] + jnp.exp(e_log - m_t) * inter[hh]
            h_ref[rows_of(c), lanes_of(hh)] = tot[:, :dh] / jnp.maximum(
                jnp.abs(tot[:, dh:dh + 1]), jnp.exp(-m_t))
            m_new = jnp.maximum(g_h + m_prev, m_loc)
            state[hh] = (jnp.exp(g_h + m_prev - m_new) * cn_prev
                         + jnp.exp(m_loc - m_new) * cn_loc[c, hh], m_new)
    for hh in range(hh_n):
        c_ref[hh] = state[hh][0]
        m_ref[hh] = jnp.broadcast_to(state[hh][1], (1, dh))
    finish(slice(0, tm))


def _mlstm_call(x, hb, acc, npost, wc, woz, wg, convw, b_i, b_f, mnorm, wb, wout, *, tm):
    bsz, s, _ = x.shape
    assert s % tm == 0 and tm % MLSTM_CHUNK == 0
    kern = functools.partial(_mlstm_kernel, tm=tm)
    row_spec = pl.BlockSpec((None, tm, D_MODEL), lambda b, i: (b, i, 0))
    consts = (npost, wc, woz, wg, convw, b_i, b_f, mnorm, wb, wout)
    return pl.pallas_call(
        kern,
        out_shape=jax.ShapeDtypeStruct(x.shape, F32),
        grid=(bsz, s // tm),
        in_specs=[row_spec, row_spec, row_spec] + [_const_spec(c.shape) for c in consts],
        out_specs=row_spec,
        scratch_shapes=[pltpu.VMEM((CONV_HALO + tm, 2 * WIDTH), F32),
                        pltpu.VMEM((tm, WIDTH), F32),
                        pltpu.VMEM((MLSTM_HEADS, MLSTM_HEAD_DIM, 2 * MLSTM_HEAD_DIM), F32),
                        pltpu.VMEM((MLSTM_HEADS, 1, MLSTM_HEAD_DIM), F32),
                        pltpu.VMEM((tm, 2 * WIDTH), F32),
                        pltpu.VMEM((tm, D_MODEL), F32)],
        compiler_params=pltpu.CompilerParams(
            dimension_semantics=("arbitrary", "arbitrary"), vmem_limit_bytes=VMEM_LIMIT_BYTES),
        name="mlstm_merge_out",
    )(x, hb, acc, *consts)


def _s5_params(a_re, a_im, log_dt, b_re, b_im, c_re, c_im):
    lam = lax.complex(a_re.astype(F32), a_im.astype(F32))
    dt = jnp.exp(log_dt.astype(F32))[:, None]
    a_bar = jnp.exp(lam * dt)
    b_bar = ((a_bar - 1.0) / lam)[..., None] * lax.complex(b_re.astype(F32), b_im.astype(F32))
    eye = jnp.eye(S5_HALF_GROUPS, dtype=F32)

    def block_in(m):
        return jnp.einsum('gnp,gh->gphn', m, eye).reshape(S5_HALF_IN, S5_HALF_STATE)

    def block_out(m):
        return jnp.einsum('gpn,gh->gnhp', m, eye).reshape(S5_HALF_STATE, S5_HALF_IN)

    bblk, cblk = [], []
    for half in range(2):
        gs = slice(half * S5_HALF_GROUPS, (half + 1) * S5_HALF_GROUPS)
        bblk.append(jnp.concatenate([block_in(jnp.real(b_bar[gs])), block_in(jnp.imag(b_bar[gs]))], axis=1))
        cblk.append(jnp.concatenate([block_out(c_re[gs].astype(F32)), block_out(-c_im[gs].astype(F32))], axis=0))
    bblk = jnp.stack(bblk).astype(BF16)
    cblk = jnp.stack(cblk).astype(BF16)
    are = jnp.real(a_bar).reshape(1, S5_GROUPS * S5_STATE)
    aim = jnp.imag(a_bar).reshape(1, S5_GROUPS * S5_STATE)
    return bblk, cblk, are, aim


def _pad_lanes(v, n):
    return jnp.pad(v.astype(F32), (0, n - v.shape[0])).reshape(1, n)


def _pad_rows(w, n):
    return jnp.pad(w.astype(F32), ((0, n - w.shape[0]), (0, 0)))


def _forward(x, norm_pre_w, norm_post_w, w_in, s5_A_re, s5_A_im, s5_log_dt, s5_B_re, s5_B_im,
             s5_C_re, s5_C_im, s5_D, s5_w_glu, pool_w, pool_scale, sconv_w, mlstm_conv_w,
             mlstm_b_i, mlstm_b_f, mlstm_norm_w, w_branch, w_out, *, tt, tm_ps, tm_ml):
    depth = w_in.shape[0]
    row = lambda v: v.astype(F32).reshape(1, -1)
    for l in range(depth):
        wl = w_in[l]
        gate_w = lambda bi: wl[:, O_GATE + bi * D_MODEL:O_GATE + (bi + 1) * D_MODEL]
        npre = row(norm_pre_w[l])
        bblk, cblk, are, aim = _s5_params(s5_A_re[l], s5_A_im[l], s5_log_dt[l], s5_B_re[l],
                                          s5_B_im[l], s5_C_re[l], s5_C_im[l])
        acc, hb = _s5_call(
            x, npre, wl[:, O_S5U:O_PU].astype(BF16), gate_w(0).astype(BF16),
            bblk, cblk, are, aim, row(s5_D[l]), s5_w_glu[l].astype(BF16),
            w_branch[l, 0].astype(BF16), tt=tt)
        acc = _pool_sconv_call(
            hb, acc, wl[:, O_PU:O_QK].astype(BF16),
            jnp.concatenate([gate_w(1), gate_w(2)], axis=1).astype(BF16),
            pool_w[l].astype(BF16), row(pool_scale[l]), _pad_rows(sconv_w[l], 8),
            w_branch[l, 1:3].astype(BF16), tm=tm_ps)
        zero_gate = jnp.zeros((D_MODEL, GATE_LANES - MLSTM_HEADS), w_in.dtype)
        wc = jnp.concatenate([wl[:, O_QK:O_O], wl[:, O_I:O_F], zero_gate,
                              wl[:, O_F:O_Z], zero_gate], axis=1).astype(BF16)
        woz = jnp.concatenate([wl[:, O_O:O_I], wl[:, O_Z:O_GATE]], axis=1).astype(BF16)
        x = _mlstm_call(
            x, hb, acc, row(norm_post_w[l]), wc, woz, gate_w(3).astype(BF16),
            _pad_rows(mlstm_conv_w[l], 8), _pad_lanes(mlstm_b_i[l], GATE_LANES),
            _pad_lanes(mlstm_b_f[l], GATE_LANES), row(mlstm_norm_w[l]),
            w_branch[l, 3].astype(BF16), w_out[l].astype(BF16), tm=tm_ml)
    return x


def kernel(x, norm_pre_w, norm_post_w, w_in, s5_A_re, s5_A_im, s5_log_dt, s5_B_re, s5_B_im, s5_C_re, s5_C_im, s5_D, s5_w_glu, pool_w, pool_scale, sconv_w, mlstm_conv_w, mlstm_b_i, mlstm_b_f, mlstm_norm_w, w_branch, w_out):
    return _forward(x, norm_pre_w, norm_post_w, w_in, s5_A_re, s5_A_im, s5_log_dt, s5_B_re,
                    s5_B_im, s5_C_re, s5_C_im, s5_D, s5_w_glu, pool_w, pool_scale, sconv_w,
                    mlstm_conv_w, mlstm_b_i, mlstm_b_f, mlstm_norm_w, w_branch, w_out,
                    tt=32, tm_ps=512, tm_ml=512)
```

```python
import functools

import jax
import jax.numpy as jnp
import numpy as np
from jax import lax
from jax.experimental import pallas as pl
from jax.experimental.pallas import tpu as pltpu

F32 = jnp.float32
BF16 = jnp.bfloat16

D_MODEL = 1024
N_BRANCHES = 4
WIDTH = 512
RMS_EPS = 1e-6
S5_GROUP = 16
S5_GROUPS = WIDTH // S5_GROUP
S5_STATE = 64
S5_HALF_GROUPS = 16
S5_HALF_IN = S5_HALF_GROUPS * S5_GROUP
S5_HALF_STATE = S5_HALF_GROUPS * S5_STATE
POOL_WINDOWS = (2, 4, 8, 16)
POOL_GROUP = WIDTH // len(POOL_WINDOWS)
POOL_HALO = 16
SCONV_WIDTH = 3
CONV_HALO = 8
MLSTM_HEADS = 4
MLSTM_HEAD_DIM = WIDTH // MLSTM_HEADS
MLSTM_CONV_WIDTH = 4
MLSTM_CHUNK = 256
LANES = 128
GATE_LANES = LANES

IN_SIZES = (WIDTH, WIDTH, WIDTH, WIDTH, WIDTH, WIDTH, WIDTH, WIDTH,
            2 * WIDTH, WIDTH, WIDTH, MLSTM_HEADS, MLSTM_HEADS, WIDTH, N_BRANCHES * D_MODEL)
_OFF = np.concatenate([[0], np.cumsum(IN_SIZES)]).tolist()
(O_S5U, O_S5Z, O_PU, O_PZ, O_SCX, O_SCB, O_SCC, O_SCZ,
 O_QK, O_V, O_O, O_I, O_F, O_Z, O_GATE, O_END) = _OFF

VMEM_LIMIT_BYTES = 56 * 1024 * 1024


def _rms(x, w):
    ms = jnp.mean(x * x, axis=-1, keepdims=True)
    return x * lax.rsqrt(ms + RMS_EPS) * w


def _sigmoid(x):
    return 0.5 * jnp.tanh(0.5 * x) + 0.5


def _silu(x):
    return x * _sigmoid(x)


def _gelu_tanh(x):
    c = np.float32(np.sqrt(2.0 / np.pi))
    return x * (0.5 * (1.0 + jnp.tanh(c * (x + 0.044715 * (x * x * x)))))


def _log_sigmoid(x):
    return jnp.minimum(x, 0.0) - jnp.log1p(jnp.exp(-jnp.abs(x)))


def _dot(a, b):
    return jnp.dot(a, b, preferred_element_type=F32)


def _layer_spec(stacked, layer):
    shape = stacked.shape[1:]
    return pl.BlockSpec((None,) + shape, lambda *_: (layer,) + (0,) * len(shape),
                        pipeline_mode=pl.Buffered(1))


def _s5_kernel(x_ref, npre_ref, wuz_ref, wg_ref, bblk_ref, cblk_ref, are_ref, aim_ref,
               dskip_ref, wglu_ref, wb_ref, out_ref, hb_out_ref, bu_ref, st_ref, gate_ref, us_ref,
               ys_ref,
               *, nb, tt, pitch, pitch_t, scan_lanes):
    r = nb * tt

    @pl.when(pl.program_id(0) == 0)
    def _():
        st_ref[...] = jnp.zeros_like(st_ref)

    x = x_ref[...].reshape(r, D_MODEL)
    hb = _rms(x, npre_ref[...]).astype(BF16)
    hb_out_ref[...] = hb.reshape(nb, tt, D_MODEL)
    uz = _dot(hb, wuz_ref[...])
    u = uz[:, :WIDTH]
    z = uz[:, WIDTH:]

    n_u = WIDTH // LANES
    for b in range(nb):
        for k in range(n_u):
            us_ref[k, b * pitch:b * pitch + tt, :] = u[b * tt:(b + 1) * tt, k * LANES:(k + 1) * LANES]
    u_tb = jnp.concatenate(
        [jnp.concatenate([us_ref[k, pl.ds(t, nb, stride=pitch), :] for k in range(n_u)], axis=1)
         for t in range(tt)], axis=0)
    ub = u_tb.astype(BF16)

    def b_stage(half):
        cols = slice(half * 2 * S5_HALF_STATE, (half + 1) * 2 * S5_HALF_STATE)
        bu_ref[:, cols] = _dot(ub[:, half * S5_HALF_IN:(half + 1) * S5_HALF_IN], bblk_ref[half])

    def scan_stage(half):
        for c0 in range(0, S5_HALF_STATE, scan_lanes):
            re0 = half * 2 * S5_HALF_STATE + c0
            im0 = re0 + S5_HALF_STATE
            s0 = half * S5_HALF_STATE + c0
            ar = jnp.broadcast_to(are_ref[:, s0:s0 + scan_lanes], (nb, scan_lanes))
            ai = jnp.broadcast_to(aim_ref[:, s0:s0 + scan_lanes], (nb, scan_lanes))
            xr = st_ref[:, re0:re0 + scan_lanes]
            xi = st_ref[:, im0:im0 + scan_lanes]
            for t in range(tt):
                rows = slice(t * nb, (t + 1) * nb)
                nr = ar * xr - ai * xi + bu_ref[rows, re0:re0 + scan_lanes]
                ni = ar * xi + ai * xr + bu_ref[rows, im0:im0 + scan_lanes]
                bu_ref[rows, re0:re0 + scan_lanes] = nr
                bu_ref[rows, im0:im0 + scan_lanes] = ni
                xr, xi = nr, ni
            st_ref[:, re0:re0 + scan_lanes] = xr
            st_ref[:, im0:im0 + scan_lanes] = xi

    def c_stage(half):
        cols = slice(half * 2 * S5_HALF_STATE, (half + 1) * 2 * S5_HALF_STATE)
        return _dot(bu_ref[:, cols].astype(BF16), cblk_ref[half])

    b_stage(0)
    b_stage(1)
    gate_ref[...] = _sigmoid(_dot(hb, wg_ref[...]))
    scan_stage(0)
    y0 = c_stage(0)
    scan_stage(1)
    y1 = c_stage(1)
    y_tb = jnp.concatenate([y0, y1], axis=1) + dskip_ref[...] * u_tb

    for t in range(tt):
        for k in range(n_u):
            ys_ref[k, t * pitch_t:t * pitch_t + nb, :] = y_tb[t * nb:(t + 1) * nb, k * LANES:(k + 1) * LANES]
    y = jnp.concatenate(
        [jnp.concatenate([ys_ref[k, pl.ds(b, tt, stride=pitch_t), :] for k in range(n_u)], axis=1)
         for b in range(nb)], axis=0)
    y = _gelu_tanh(y)
    y = y * _sigmoid(_dot(y.astype(BF16), wglu_ref[...]))
    pb = _dot((y * _silu(z)).astype(BF16), wb_ref[...])
    out_ref[...] = (gate_ref[...] * pb).reshape(nb, tt, D_MODEL)


def _odd_tile_pitch(n):
    return n if (n // 8) % 2 == 1 else n + 8


def _s5_call(x, npre, wuz, wg, bblk, cblk, are, aim, dskip, wglu, wb, *, layer, tt, scan_lanes=512):
    nb, s, _ = x.shape
    assert s % tt == 0 and tt % 8 == 0 and nb % 8 == 0
    pitch, pitch_t = _odd_tile_pitch(tt), _odd_tile_pitch(nb)
    kern = functools.partial(_s5_kernel, nb=nb, tt=tt, pitch=pitch, pitch_t=pitch_t,
                             scan_lanes=scan_lanes)
    row_spec = pl.BlockSpec((nb, tt, D_MODEL), lambda i: (0, i, 0))
    consts = (npre, wuz, wg, bblk, cblk, are, aim, dskip, wglu, wb)
    return pl.pallas_call(
        kern,
        out_shape=(jax.ShapeDtypeStruct(x.shape, F32), jax.ShapeDtypeStruct(x.shape, BF16)),
        grid=(s // tt,),
        in_specs=[row_spec] + [_layer_spec(c, layer) for c in consts],
        out_specs=(row_spec, row_spec),
        scratch_shapes=[pltpu.VMEM((nb * tt, 4 * S5_HALF_STATE), F32),
                        pltpu.VMEM((nb, 4 * S5_HALF_STATE), F32),
                        pltpu.VMEM((nb * tt, D_MODEL), F32),
                        pltpu.VMEM((WIDTH // LANES, nb * pitch, LANES), F32),
                        pltpu.VMEM((WIDTH // LANES, tt * pitch_t, LANES), F32)],
        compiler_params=pltpu.CompilerParams(
            dimension_semantics=("arbitrary",), vmem_limit_bytes=VMEM_LIMIT_BYTES),
        name="s5_branch",
    )(x, *consts)


def _pool_sconv_kernel(hb_ref, acc_ref, wp_ref, wg_ref, poolw_ref, pscale_ref, scw_ref,
                       wb_ref, out_ref, pu_ref, cx_ref, gate_ref, *, tm):
    s_idx = pl.program_id(1)

    @pl.when(s_idx == 0)
    def _():
        pu_ref[0:POOL_HALO, :] = jnp.zeros((POOL_HALO, WIDTH), F32)
        cx_ref[0:CONV_HALO, :] = jnp.zeros((CONV_HALO, WIDTH), F32)

    @pl.when(s_idx > 0)
    def _():
        pu_ref[0:POOL_HALO, :] = pu_ref[tm:tm + POOL_HALO, :]
        cx_ref[0:CONV_HALO, :] = cx_ref[tm:tm + CONV_HALO, :]

    hb = hb_ref[...]
    proj = _dot(hb, wp_ref[...])
    gate_ref[...] = _sigmoid(_dot(hb, wg_ref[...]))
    pool_u = proj[:, 0 * WIDTH:1 * WIDTH]
    pool_z = proj[:, 1 * WIDTH:2 * WIDTH]
    sc_x = proj[:, 2 * WIDTH:3 * WIDTH]
    sc_b = proj[:, 3 * WIDTH:4 * WIDTH]
    sc_c = proj[:, 4 * WIDTH:5 * WIDTH]
    sc_z = proj[:, 5 * WIDTH:6 * WIDTH]

    pu_ref[POOL_HALO:POOL_HALO + tm, :] = pool_u
    t_glob = s_idx * tm + lax.broadcasted_iota(jnp.int32, (tm, 1), 0)
    mixed = []
    for gi, win in enumerate(POOL_WINDOWS):
        lanes = slice(gi * POOL_GROUP, (gi + 1) * POOL_GROUP)
        wsum = pu_ref[POOL_HALO:POOL_HALO + tm, lanes]
        for j in range(1, win):
            wsum = wsum + pu_ref[POOL_HALO - j:POOL_HALO - j + tm, lanes]
        cnt = jnp.minimum(t_glob + 1, win).astype(F32)
        pooled = wsum / cnt - pool_u[:, lanes]
        mixed.append(_dot(pooled.astype(BF16), poolw_ref[gi]))
    y_b = jnp.concatenate(mixed, axis=1) * pscale_ref[...]

    cx = sc_c * sc_x
    cx_ref[CONV_HALO:CONV_HALO + tm, :] = cx
    conv = scw_ref[SCONV_WIDTH - 1:SCONV_WIDTH, :] * cx
    for k in range(SCONV_WIDTH - 1):
        sh = SCONV_WIDTH - 1 - k
        conv = conv + scw_ref[k:k + 1, :] * cx_ref[CONV_HALO - sh:CONV_HALO - sh + tm, :]
    y_c = sc_b * conv

    p_b = _dot((y_b * _silu(pool_z)).astype(BF16), wb_ref[0])
    p_c = _dot((y_c * _silu(sc_z)).astype(BF16), wb_ref[1])
    gates = gate_ref[...]
    out_ref[...] = acc_ref[...] + gates[:, :D_MODEL] * p_b + gates[:, D_MODEL:] * p_c


def _pool_sconv_call(hb, acc, wp, wg, poolw, pscale, scw, wb, *, layer, tm):
    bsz, s, _ = acc.shape
    assert s % tm == 0 and tm % 8 == 0 and tm >= POOL_HALO
    kern = functools.partial(_pool_sconv_kernel, tm=tm)
    row_spec = pl.BlockSpec((None, tm, D_MODEL), lambda b, i: (b, i, 0))
    consts = (wp, wg, poolw, pscale, scw, wb)
    return pl.pallas_call(
        kern,
        out_shape=jax.ShapeDtypeStruct(acc.shape, F32),
        grid=(bsz, s // tm),
        in_specs=[row_spec, row_spec] + [_layer_spec(c, layer) for c in consts],
        out_specs=row_spec,
        scratch_shapes=[pltpu.VMEM((POOL_HALO + tm, WIDTH), F32),
                        pltpu.VMEM((CONV_HALO + tm, WIDTH), F32),
                        pltpu.VMEM((tm, 2 * D_MODEL), F32)],
        compiler_params=pltpu.CompilerParams(
            dimension_semantics=("arbitrary", "arbitrary"), vmem_limit_bytes=VMEM_LIMIT_BYTES),
        name="pool_sconv_branch",
    )(hb, acc, *consts)


def _mlstm_kernel(x_ref, hb_ref, acc_ref, npost_ref, wc_ref, woz_ref, wg_ref, convw_ref, bi_ref,
                  bf_ref, mnorm_ref, wb_ref, wout_ref, out_ref,
                  qk_ref, h_ref, c_ref, m_ref, oz_ref, gate_ref, *, tm):
    s_idx = pl.program_id(1)
    hh_n, dh, ll = MLSTM_HEADS, MLSTM_HEAD_DIM, MLSTM_CHUNK

    @pl.when(s_idx == 0)
    def _():
        qk_ref[0:CONV_HALO, :] = jnp.zeros((CONV_HALO, 2 * WIDTH), F32)
        c_ref[...] = jnp.zeros_like(c_ref)
        m_ref[...] = jnp.zeros_like(m_ref)

    @pl.when(s_idx > 0)
    def _():
        qk_ref[0:CONV_HALO, :] = qk_ref[tm:tm + CONV_HALO, :]

    hb = hb_ref[...]
    proj = _dot(hb, wc_ref[...])
    oz_ref[...] = _dot(hb, woz_ref[...])
    v_all = proj[:, 2 * WIDTH:3 * WIDTH]
    i_log = proj[:, 3 * WIDTH:3 * WIDTH + GATE_LANES] + bi_ref[...]
    log_f = _log_sigmoid(proj[:, 3 * WIDTH + GATE_LANES:3 * WIDTH + 2 * GATE_LANES] + bf_ref[...])

    qk_ref[CONV_HALO:CONV_HALO + tm, :] = proj[:, :2 * WIDTH]
    conv = convw_ref[MLSTM_CONV_WIDTH - 1:MLSTM_CONV_WIDTH, :] * proj[:, :2 * WIDTH]
    for k in range(MLSTM_CONV_WIDTH - 1):
        sh = MLSTM_CONV_WIDTH - 1 - k
        conv = conv + convw_ref[k:k + 1, :] * qk_ref[CONV_HALO - sh:CONV_HALO - sh + tm, :]
    qk_act = _silu(conv)
    q_all = qk_act[:, :WIDTH] * np.float32(dh ** -0.5)
    k_all = qk_act[:, WIDTH:]

    row_i = lax.broadcasted_iota(jnp.int32, (ll, ll), 0)
    col_i = lax.broadcasted_iota(jnp.int32, (ll, ll), 1)
    causal = col_i <= row_i
    tri = causal.astype(BF16)

    ones_blk = (lax.broadcasted_iota(jnp.int32, (ll, dh), 1) == 0).astype(BF16)

    blocks = [(c, hh) for c in range(tm // ll) for hh in range(hh_n)]
    rows_of = lambda c: slice(c * ll, (c + 1) * ll)
    lanes_of = lambda hh: slice(hh * dh, (hh + 1) * dh)
    b_cs, s_qk, qhb = {}, {}, {}
    for c in range(tm // ll):
        lf_c = log_f[rows_of(c)]
        lf_hi = lf_c.astype(BF16)
        lf_lo = (lf_c - lf_hi.astype(F32)).astype(BF16)
        b_cs[c] = _dot(tri, lf_hi) + _dot(tri, lf_lo)
    for c, hh in blocks:
        qhb[c, hh] = q_all[rows_of(c), lanes_of(hh)].astype(BF16)
        s_qk[c, hh] = lax.dot_general(qhb[c, hh], k_all[rows_of(c), lanes_of(hh)].astype(BF16),
                                      (((1,), (1,)), ((), ())), preferred_element_type=F32)
    gate_ref[...] = _sigmoid(_dot(hb, wg_ref[...]))

    stage_b = {}
    for c in range(tm // ll):
        r_col = i_log[rows_of(c)] - b_cs[c]
        r_row = r_col.T
        g_tot = b_cs[c][ll - 1:ll, :]
        a_col = g_tot + r_col
        for hh in range(hh_n):
            b_h = b_cs[c][:, hh:hh + 1]
            r_msk = jnp.where(causal, r_row[hh:hh + 1, :], -jnp.inf)
            r_max = jnp.max(r_msk, axis=1, keepdims=True)
            m_in = b_h + r_max
            w_un = (jnp.exp(r_msk - r_max) * s_qk[c, hh]).astype(BF16)
            a_h = a_col[:, hh:hh + 1]
            m_loc = jnp.max(a_h, axis=0, keepdims=True)
            kw = (jnp.exp(a_h - m_loc) * k_all[rows_of(c), lanes_of(hh)]).astype(BF16)
            stage_b[c, hh] = (b_h, m_in, w_un, m_loc, kw, g_tot[:, hh:hh + 1])

    nd_in, cn_loc = {}, {}
    for c, hh in blocks:
        _, _, w_un, _, kw, _ = stage_b[c, hh]
        vext = jnp.concatenate([v_all[rows_of(c), lanes_of(hh)].astype(BF16), ones_blk], axis=1)
        nd_in[c, hh] = _dot(w_un, vext)
        cn_loc[c, hh] = lax.dot_general(kw, vext, (((0,), (0,)), ((), ())),
                                        preferred_element_type=F32)

    def finish(rows):
        hg = h_ref[rows, :] * _sigmoid(oz_ref[rows, :WIDTH])
        normed = []
        for hh in range(hh_n):
            blk = hg[:, hh * dh:(hh + 1) * dh]
            normed.append(blk * lax.rsqrt(jnp.mean(blk * blk, axis=-1, keepdims=True) + RMS_EPS))
        y_d = jnp.concatenate(normed, axis=1) * mnorm_ref[...]
        p_d = _dot((y_d * _silu(oz_ref[rows, WIDTH:])).astype(BF16), wb_ref[...])
        merged = acc_ref[rows, :] + gate_ref[rows, :] * p_d
        out = _dot(merged.astype(BF16), wout_ref[...])
        out_ref[rows, :] = x_ref[rows, :] + _rms(out, npost_ref[...])

    state = [(c_ref[hh], m_ref[hh][:, 0:1]) for hh in range(hh_n)]
    for c in range(tm // ll):
        inter = [_dot(qhb[c, hh], state[hh][0].astype(BF16)) for hh in range(hh_n)]
        for hh in range(hh_n):
            b_h, m_in, _, m_loc, _, g_h = stage_b[c, hh]
            cn_prev, m_prev = state[hh]
            e_log = b_h + m_prev
            m_t = jnp.maximum(m_in, e_log)
            tot = jnp.exp(m_in - m_t) * nd_in[c, hh] + jnp.exp(e_log - m_t) * inter[hh]
            h_ref[rows_of(c), lanes_of(hh)] = tot[:, :dh] / jnp.maximum(
                jnp.abs(tot[:, dh:dh + 1]), jnp.exp(-m_t))
            m_new = jnp.maximum(g_h + m_prev, m_loc)
            state[hh] = (jnp.exp(g_h + m_prev - m_new) * cn_prev
                         + jnp.exp(m_loc - m_new) * cn_loc[c, hh], m_new)
    for hh in range(hh_n):
        c_ref[hh] = state[hh][0]
        m_ref[hh] = jnp.broadcast_to(state[hh][1], (1, dh))
    finish(slice(0, tm))


def _mlstm_call(x, hb, acc, npost, wc, woz, wg, convw, b_i, b_f, mnorm, wb, wout, *, layer, tm):
    bsz, s, _ = x.shape
    assert s % tm == 0 and tm % MLSTM_CHUNK == 0
    kern = functools.partial(_mlstm_kernel, tm=tm)
    row_spec = pl.BlockSpec((None, tm, D_MODEL), lambda b, i: (b, i, 0))
    consts = (npost, wc, woz, wg, convw, b_i, b_f, mnorm, wb, wout)
    return pl.pallas_call(
        kern,
        out_shape=jax.ShapeDtypeStruct(x.shape, F32),
        grid=(bsz, s // tm),
        in_specs=[row_spec, row_spec, row_spec] + [_layer_spec(c, layer) for c in consts],
        out_specs=row_spec,
        scratch_shapes=[pltpu.VMEM((CONV_HALO + tm, 2 * WIDTH), F32),
                        pltpu.VMEM((tm, WIDTH), F32),
                        pltpu.VMEM((MLSTM_HEADS, MLSTM_HEAD_DIM, 2 * MLSTM_HEAD_DIM), F32),
                        pltpu.VMEM((MLSTM_HEADS, 1, MLSTM_HEAD_DIM), F32),
                        pltpu.VMEM((tm, 2 * WIDTH), F32),
                        pltpu.VMEM((tm, D_MODEL), F32)],
        compiler_params=pltpu.CompilerParams(
            dimension_semantics=("arbitrary", "arbitrary"), vmem_limit_bytes=VMEM_LIMIT_BYTES),
        name="mlstm_merge_out",
    )(x, hb, acc, *consts)


def _s5_params(a_re, a_im, log_dt, b_re, b_im, c_re, c_im):
    depth = a_re.shape[0]
    lam = lax.complex(a_re.astype(F32), a_im.astype(F32))
    dt = jnp.exp(log_dt.astype(F32))[..., None]
    a_bar = jnp.exp(lam * dt)
    b_bar = ((a_bar - 1.0) / lam)[..., None] * lax.complex(b_re.astype(F32), b_im.astype(F32))
    eye = jnp.eye(S5_HALF_GROUPS, dtype=F32)
    halves = lambda m: m.reshape(depth, 2, S5_HALF_GROUPS, *m.shape[2:])

    def block_in(m):
        return jnp.einsum('lzgnp,gh->lzgphn', m, eye).reshape(depth, 2, S5_HALF_IN, S5_HALF_STATE)

    def block_out(m):
        return jnp.einsum('lzgpn,gh->lzgnhp', m, eye).reshape(depth, 2, S5_HALF_STATE, S5_HALF_IN)

    bblk = jnp.concatenate([block_in(halves(jnp.real(b_bar))), block_in(halves(jnp.imag(b_bar)))],
                           axis=3).astype(BF16)
    cblk = jnp.concatenate([block_out(halves(c_re.astype(F32))), block_out(halves(-c_im.astype(F32)))],
                           axis=2).astype(BF16)
    are = jnp.real(a_bar).reshape(depth, 1, S5_GROUPS * S5_STATE)
    aim = jnp.imag(a_bar).reshape(depth, 1, S5_GROUPS * S5_STATE)
    return bblk, cblk, are, aim


def _forward(x, norm_pre_w, norm_post_w, w_in, s5_A_re, s5_A_im, s5_log_dt, s5_B_re, s5_B_im,
             s5_C_re, s5_C_im, s5_D, s5_w_glu, pool_w, pool_scale, sconv_w, mlstm_conv_w,
             mlstm_b_i, mlstm_b_f, mlstm_norm_w, w_branch, w_out, *, tt, tm_ps, tm_ml):
    depth = w_in.shape[0]
    rows = lambda v: v.astype(F32).reshape(depth, 1, -1)
    cols = lambda a, b: w_in[:, :, a:b]
    gate_w = lambda lo, hi: cols(O_GATE + lo * D_MODEL, O_GATE + hi * D_MODEL).astype(BF16)
    pad_lanes = lambda v: jnp.pad(v.astype(F32), ((0, 0), (0, GATE_LANES - v.shape[1])))[:, None, :]
    pad_rows = lambda w: jnp.pad(w.astype(F32), ((0, 0), (0, 8 - w.shape[1]), (0, 0)))
    zero_gate = jnp.zeros((depth, D_MODEL, GATE_LANES - MLSTM_HEADS), w_in.dtype)
    npre, npost = rows(norm_pre_w), rows(norm_post_w)
    s5_consts = (npre, cols(O_S5U, O_PU).astype(BF16), gate_w(0, 1),
                 *_s5_params(s5_A_re, s5_A_im, s5_log_dt, s5_B_re, s5_B_im, s5_C_re, s5_C_im),
                 rows(s5_D), s5_w_glu.astype(BF16), w_branch[:, 0].astype(BF16))
    ps_consts = (cols(O_PU, O_QK).astype(BF16), gate_w(1, 3), pool_w.astype(BF16), rows(pool_scale),
                 pad_rows(sconv_w), w_branch[:, 1:3].astype(BF16))
    ml_consts = (npost,
                 jnp.concatenate([cols(O_QK, O_O), cols(O_I, O_F), zero_gate, cols(O_F, O_Z), zero_gate],
                                 axis=2).astype(BF16),
                 jnp.concatenate([cols(O_O, O_I), cols(O_Z, O_GATE)], axis=2).astype(BF16),
                 gate_w(3, 4), pad_rows(mlstm_conv_w), pad_lanes(mlstm_b_i), pad_lanes(mlstm_b_f),
                 rows(mlstm_norm_w), w_branch[:, 3].astype(BF16), w_out.astype(BF16))
    for l in range(depth):
        acc, hb = _s5_call(x, *s5_consts, layer=l, tt=tt)
        acc = _pool_sconv_call(hb, acc, *ps_consts, layer=l, tm=tm_ps)
        x = _mlstm_call(x, hb, acc, *ml_consts, layer=l, tm=tm_ml)
    return x


def kernel(x, norm_pre_w, norm_post_w, w_in, s5_A_re, s5_A_im, s5_log_dt, s5_B_re, s5_B_im, s5_C_re, s5_C_im, s5_D, s5_w_glu, pool_w, pool_scale, sconv_w, mlstm_conv_w, mlstm_b_i, mlstm_b_f, mlstm_norm_w, w_branch, w_out):
    return _forward(x, norm_pre_w, norm_post_w, w_in, s5_A_re, s5_A_im, s5_log_dt, s5_B_re,
                    s5_B_im, s5_C_re, s5_C_im, s5_D, s5_w_glu, pool_w, pool_scale, sconv_w,
                    mlstm_conv_w, mlstm_b_i, mlstm_b_f, mlstm_norm_w, w_branch, w_out,
                    tt=32, tm_ps=1024, tm_ml=512)
```

```python
import functools

import jax
import jax.numpy as jnp
import numpy as np
from jax import lax
from jax.experimental import pallas as pl
from jax.experimental.pallas import tpu as pltpu

F32 = jnp.float32
BF16 = jnp.bfloat16

D_MODEL = 1024
N_BRANCHES = 4
WIDTH = 512
RMS_EPS = 1e-6
S5_GROUP = 16
S5_GROUPS = WIDTH // S5_GROUP
S5_STATE = 64
S5_HALF_GROUPS = 16
S5_HALF_IN = S5_HALF_GROUPS * S5_GROUP
S5_HALF_STATE = S5_HALF_GROUPS * S5_STATE
POOL_WINDOWS = (2, 4, 8, 16)
POOL_GROUP = WIDTH // len(POOL_WINDOWS)
POOL_HALO = 16
SCONV_WIDTH = 3
CONV_HALO = 8
MLSTM_HEADS = 4
MLSTM_HEAD_DIM = WIDTH // MLSTM_HEADS
MLSTM_CONV_WIDTH = 4
MLSTM_CHUNK = 256
LANES = 128
GATE_LANES = LANES

IN_SIZES = (WIDTH, WIDTH, WIDTH, WIDTH, WIDTH, WIDTH, WIDTH, WIDTH,
            2 * WIDTH, WIDTH, WIDTH, MLSTM_HEADS, MLSTM_HEADS, WIDTH, N_BRANCHES * D_MODEL)
_OFF = np.concatenate([[0], np.cumsum(IN_SIZES)]).tolist()
(O_S5U, O_S5Z, O_PU, O_PZ, O_SCX, O_SCB, O_SCC, O_SCZ,
 O_QK, O_V, O_O, O_I, O_F, O_Z, O_GATE, O_END) = _OFF

VMEM_LIMIT_BYTES = 56 * 1024 * 1024


def _rms(x, w):
    ms = jnp.mean(x * x, axis=-1, keepdims=True)
    return x * lax.rsqrt(ms + RMS_EPS) * w


def _sigmoid(x):
    return 0.5 * jnp.tanh(0.5 * x) + 0.5


def _silu(x):
    return x * _sigmoid(x)


def _gelu_tanh(x):
    c = np.float32(np.sqrt(2.0 / np.pi))
    return x * (0.5 * (1.0 + jnp.tanh(c * (x + 0.044715 * (x * x * x)))))


def _log_sigmoid(x):
    return jnp.minimum(x, 0.0) - jnp.log1p(jnp.exp(-jnp.abs(x)))


def _dot(a, b):
    return jnp.dot(a, b, preferred_element_type=F32)


def _layer_spec(stacked, layer):
    shape = stacked.shape[1:]
    return pl.BlockSpec((None,) + shape, lambda *_: (layer,) + (0,) * len(shape),
                        pipeline_mode=pl.Buffered(1))


def _s5_kernel(x_ref, npre_ref, wuz_ref, wg_ref, bblk_ref, cblk_ref, are_ref, aim_ref,
               dskip_ref, wglu_ref, wb_ref, out_ref, hb_out_ref, bu_ref, st_ref, gate_ref, us_ref,
               ys_ref, xs_ref,
               *, nb, tt, pitch, pitch_t, scan_lanes):
    r = nb * tt

    @pl.when(pl.program_id(0) == 0)
    def _():
        st_ref[...] = jnp.zeros_like(st_ref)

    x = x_ref[...].reshape(r, D_MODEL)
    hb = _rms(x, npre_ref[...]).astype(BF16)
    hb_out_ref[...] = hb.reshape(nb, tt, D_MODEL)
    uz = _dot(hb, wuz_ref[...])
    u = uz[:, :WIDTH]
    z = uz[:, WIDTH:]

    n_u = WIDTH // LANES
    for b in range(nb):
        for k in range(n_u):
            us_ref[k, b * pitch:b * pitch + tt, :] = u[b * tt:(b + 1) * tt, k * LANES:(k + 1) * LANES]
    u_tb = jnp.concatenate(
        [jnp.concatenate([us_ref[k, pl.ds(t, nb, stride=pitch), :] for k in range(n_u)], axis=1)
         for t in range(tt)], axis=0)
    ub = u_tb.astype(BF16)

    def b_stage(half):
        cols = slice(half * 2 * S5_HALF_STATE, (half + 1) * 2 * S5_HALF_STATE)
        bu_ref[:, cols] = _dot(ub[:, half * S5_HALF_IN:(half + 1) * S5_HALF_IN], bblk_ref[half])

    def scan_stage(half):
        for c0 in range(0, S5_HALF_STATE, scan_lanes):
            re0 = half * 2 * S5_HALF_STATE + c0
            im0 = re0 + S5_HALF_STATE
            s0 = half * S5_HALF_STATE + c0
            ar = jnp.broadcast_to(are_ref[:, s0:s0 + scan_lanes], (nb, scan_lanes))
            ai = jnp.broadcast_to(aim_ref[:, s0:s0 + scan_lanes], (nb, scan_lanes))
            xr = st_ref[:, re0:re0 + scan_lanes]
            xi = st_ref[:, im0:im0 + scan_lanes]
            for t in range(tt):
                rows = slice(t * nb, (t + 1) * nb)
                nr = ar * xr - ai * xi + bu_ref[rows, re0:re0 + scan_lanes]
                ni = ar * xi + ai * xr + bu_ref[rows, im0:im0 + scan_lanes]
                xs_ref[rows, re0:re0 + scan_lanes] = nr.astype(BF16)
                xs_ref[rows, im0:im0 + scan_lanes] = ni.astype(BF16)
                xr, xi = nr, ni
            st_ref[:, re0:re0 + scan_lanes] = xr
            st_ref[:, im0:im0 + scan_lanes] = xi

    def c_stage(half):
        cols = slice(half * 2 * S5_HALF_STATE, (half + 1) * 2 * S5_HALF_STATE)
        return _dot(xs_ref[:, cols], cblk_ref[half])

    b_stage(0)
    b_stage(1)
    gate_ref[...] = _sigmoid(_dot(hb, wg_ref[...]))
    scan_stage(0)
    y0 = c_stage(0)
    scan_stage(1)
    y1 = c_stage(1)
    y_tb = jnp.concatenate([y0, y1], axis=1) + dskip_ref[...] * u_tb

    for t in range(tt):
        for k in range(n_u):
            ys_ref[k, t * pitch_t:t * pitch_t + nb, :] = y_tb[t * nb:(t + 1) * nb, k * LANES:(k + 1) * LANES]
    y = jnp.concatenate(
        [jnp.concatenate([ys_ref[k, pl.ds(b, tt, stride=pitch_t), :] for k in range(n_u)], axis=1)
         for b in range(nb)], axis=0)
    y = _gelu_tanh(y)
    y = y * _sigmoid(_dot(y.astype(BF16), wglu_ref[...]))
    pb = _dot((y * _silu(z)).astype(BF16), wb_ref[...])
    out_ref[...] = (gate_ref[...] * pb).reshape(nb, tt, D_MODEL)


def _odd_tile_pitch(n):
    return n if (n // 8) % 2 == 1 else n + 8


def _s5_call(x, npre, wuz, wg, bblk, cblk, are, aim, dskip, wglu, wb, *, layer, tt, scan_lanes=512):
    nb, s, _ = x.shape
    assert s % tt == 0 and tt % 8 == 0 and nb % 8 == 0
    pitch, pitch_t = _odd_tile_pitch(tt), _odd_tile_pitch(nb)
    kern = functools.partial(_s5_kernel, nb=nb, tt=tt, pitch=pitch, pitch_t=pitch_t,
                             scan_lanes=scan_lanes)
    row_spec = pl.BlockSpec((nb, tt, D_MODEL), lambda i: (0, i, 0))
    consts = (npre, wuz, wg, bblk, cblk, are, aim, dskip, wglu, wb)
    return pl.pallas_call(
        kern,
        out_shape=(jax.ShapeDtypeStruct(x.shape, F32), jax.ShapeDtypeStruct(x.shape, BF16)),
        grid=(s // tt,),
        in_specs=[row_spec] + [_layer_spec(c, layer) for c in consts],
        out_specs=(row_spec, row_spec),
        scratch_shapes=[pltpu.VMEM((nb * tt, 4 * S5_HALF_STATE), F32),
                        pltpu.VMEM((nb, 4 * S5_HALF_STATE), F32),
                        pltpu.VMEM((nb * tt, D_MODEL), F32),
                        pltpu.VMEM((WIDTH // LANES, nb * pitch, LANES), F32),
                        pltpu.VMEM((WIDTH // LANES, tt * pitch_t, LANES), F32),
                        pltpu.VMEM((nb * tt, 4 * S5_HALF_STATE), BF16)],
        compiler_params=pltpu.CompilerParams(
            dimension_semantics=("arbitrary",), vmem_limit_bytes=VMEM_LIMIT_BYTES),
        name="s5_branch",
    )(x, *consts)


def _pool_sconv_kernel(hb_ref, acc_ref, wp_ref, wg_ref, poolw_ref, pscale_ref, scw_ref,
                       wb_ref, out_ref, pu_ref, cx_ref, gate_ref, *, tm):
    s_idx = pl.program_id(1)

    @pl.when(s_idx == 0)
    def _():
        pu_ref[0:POOL_HALO, :] = jnp.zeros((POOL_HALO, WIDTH), F32)
        cx_ref[0:CONV_HALO, :] = jnp.zeros((CONV_HALO, WIDTH), F32)

    @pl.when(s_idx > 0)
    def _():
        pu_ref[0:POOL_HALO, :] = pu_ref[tm:tm + POOL_HALO, :]
        cx_ref[0:CONV_HALO, :] = cx_ref[tm:tm + CONV_HALO, :]

    hb = hb_ref[...]
    proj = _dot(hb, wp_ref[...])
    gate_ref[...] = _sigmoid(_dot(hb, wg_ref[...]))
    pool_u = proj[:, 0 * WIDTH:1 * WIDTH]
    pool_z = proj[:, 1 * WIDTH:2 * WIDTH]
    sc_x = proj[:, 2 * WIDTH:3 * WIDTH]
    sc_b = proj[:, 3 * WIDTH:4 * WIDTH]
    sc_c = proj[:, 4 * WIDTH:5 * WIDTH]
    sc_z = proj[:, 5 * WIDTH:6 * WIDTH]

    pu_ref[POOL_HALO:POOL_HALO + tm, :] = pool_u
    t_glob = s_idx * tm + lax.broadcasted_iota(jnp.int32, (tm, 1), 0)
    mixed = []
    for gi, win in enumerate(POOL_WINDOWS):
        lanes = slice(gi * POOL_GROUP, (gi + 1) * POOL_GROUP)
        wsum = pu_ref[POOL_HALO:POOL_HALO + tm, lanes]
        for j in range(1, win):
            wsum = wsum + pu_ref[POOL_HALO - j:POOL_HALO - j + tm, lanes]
        cnt = jnp.minimum(t_glob + 1, win).astype(F32)
        pooled = wsum / cnt - pool_u[:, lanes]
        mixed.append(_dot(pooled.astype(BF16), poolw_ref[gi]))
    y_b = jnp.concatenate(mixed, axis=1) * pscale_ref[...]

    cx = sc_c * sc_x
    cx_ref[CONV_HALO:CONV_HALO + tm, :] = cx
    conv = scw_ref[SCONV_WIDTH - 1:SCONV_WIDTH, :] * cx
    for k in range(SCONV_WIDTH - 1):
        sh = SCONV_WIDTH - 1 - k
        conv = conv + scw_ref[k:k + 1, :] * cx_ref[CONV_HALO - sh:CONV_HALO - sh + tm, :]
    y_c = sc_b * conv

    p_b = _dot((y_b * _silu(pool_z)).astype(BF16), wb_ref[0])
    p_c = _dot((y_c * _silu(sc_z)).astype(BF16), wb_ref[1])
    gates = gate_ref[...]
    out_ref[...] = acc_ref[...] + gates[:, :D_MODEL] * p_b + gates[:, D_MODEL:] * p_c


def _pool_sconv_call(hb, acc, wp, wg, poolw, pscale, scw, wb, *, layer, tm):
    bsz, s, _ = acc.shape
    assert s % tm == 0 and tm % 8 == 0 and tm >= POOL_HALO
    kern = functools.partial(_pool_sconv_kernel, tm=tm)
    row_spec = pl.BlockSpec((None, tm, D_MODEL), lambda b, i: (b, i, 0))
    consts = (wp, wg, poolw, pscale, scw, wb)
    return pl.pallas_call(
        kern,
        out_shape=jax.ShapeDtypeStruct(acc.shape, F32),
        grid=(bsz, s // tm),
        in_specs=[row_spec, row_spec] + [_layer_spec(c, layer) for c in consts],
        out_specs=row_spec,
        scratch_shapes=[pltpu.VMEM((POOL_HALO + tm, WIDTH), F32),
                        pltpu.VMEM((CONV_HALO + tm, WIDTH), F32),
                        pltpu.VMEM((tm, 2 * D_MODEL), F32)],
        compiler_params=pltpu.CompilerParams(
            dimension_semantics=("arbitrary", "arbitrary"), vmem_limit_bytes=VMEM_LIMIT_BYTES),
        name="pool_sconv_branch",
    )(hb, acc, *consts)


def _mlstm_kernel(x_ref, hb_ref, acc_ref, npost_ref, wc_ref, woz_ref, wg_ref, convw_ref, bi_ref,
                  bf_ref, mnorm_ref, wb_ref, wout_ref, out_ref,
                  qk_ref, h_ref, c_ref, m_ref, oz_ref, gate_ref, *, tm):
    s_idx = pl.program_id(1)
    hh_n, dh, ll = MLSTM_HEADS, MLSTM_HEAD_DIM, MLSTM_CHUNK

    @pl.when(s_idx == 0)
    def _():
        qk_ref[0:CONV_HALO, :] = jnp.zeros((CONV_HALO, 2 * WIDTH), F32)
        c_ref[...] = jnp.zeros_like(c_ref)
        m_ref[...] = jnp.zeros_like(m_ref)

    @pl.when(s_idx > 0)
    def _():
        qk_ref[0:CONV_HALO, :] = qk_ref[tm:tm + CONV_HALO, :]

    hb = hb_ref[...]
    proj = _dot(hb, wc_ref[...])
    oz_ref[...] = _dot(hb, woz_ref[...])
    v_all = proj[:, 2 * WIDTH:3 * WIDTH]
    i_log = proj[:, 3 * WIDTH:3 * WIDTH + GATE_LANES] + bi_ref[...]
    log_f = _log_sigmoid(proj[:, 3 * WIDTH + GATE_LANES:3 * WIDTH + 2 * GATE_LANES] + bf_ref[...])

    qk_ref[CONV_HALO:CONV_HALO + tm, :] = proj[:, :2 * WIDTH]
    conv = convw_ref[MLSTM_CONV_WIDTH - 1:MLSTM_CONV_WIDTH, :] * proj[:, :2 * WIDTH]
    for k in range(MLSTM_CONV_WIDTH - 1):
        sh = MLSTM_CONV_WIDTH - 1 - k
        conv = conv + convw_ref[k:k + 1, :] * qk_ref[CONV_HALO - sh:CONV_HALO - sh + tm, :]
    qk_act = _silu(conv)
    q_all = qk_act[:, :WIDTH] * np.float32(dh ** -0.5)
    k_all = qk_act[:, WIDTH:]

    row_i = lax.broadcasted_iota(jnp.int32, (ll, ll), 0)
    col_i = lax.broadcasted_iota(jnp.int32, (ll, ll), 1)
    causal = col_i <= row_i
    tri = causal.astype(BF16)

    ones_blk = (lax.broadcasted_iota(jnp.int32, (ll, dh), 1) == 0).astype(BF16)

    blocks = [(c, hh) for c in range(tm // ll) for hh in range(hh_n)]
    rows_of = lambda c: slice(c * ll, (c + 1) * ll)
    lanes_of = lambda hh: slice(hh * dh, (hh + 1) * dh)
    b_cs, s_qk, qhb = {}, {}, {}
    for c in range(tm // ll):
        lf_c = log_f[rows_of(c)]
        lf_hi = lf_c.astype(BF16)
        lf_lo = (lf_c - lf_hi.astype(F32)).astype(BF16)
        b_cs[c] = _dot(tri, lf_hi) + _dot(tri, lf_lo)
    for c, hh in blocks:
        qhb[c, hh] = q_all[rows_of(c), lanes_of(hh)].astype(BF16)
        s_qk[c, hh] = lax.dot_general(qhb[c, hh], k_all[rows_of(c), lanes_of(hh)].astype(BF16),
                                      (((1,), (1,)), ((), ())), preferred_element_type=F32)
    gate_ref[...] = _sigmoid(_dot(hb, wg_ref[...]))

    stage_b = {}
    for c in range(tm // ll):
        r_col = i_log[rows_of(c)] - b_cs[c]
        r_row = r_col.T
        g_tot = b_cs[c][ll - 1:ll, :]
        a_col = g_tot + r_col
        for hh in range(hh_n):
            b_h = b_cs[c][:, hh:hh + 1]
            r_msk = jnp.where(causal, r_row[hh:hh + 1, :], -jnp.inf)
            r_max = jnp.max(r_msk, axis=1, keepdims=True)
            m_in = b_h + r_max
            w_un = (jnp.exp(r_msk - r_max) * s_qk[c, hh]).astype(BF16)
            a_h = a_col[:, hh:hh + 1]
            m_loc = jnp.max(a_h, axis=0, keepdims=True)
            kw = (jnp.exp(a_h - m_loc) * k_all[rows_of(c), lanes_of(hh)]).astype(BF16)
            stage_b[c, hh] = (b_h, m_in, w_un, m_loc, kw, g_tot[:, hh:hh + 1])

    nd_in, cn_loc = {}, {}
    for c, hh in blocks:
        _, _, w_un, _, kw, _ = stage_b[c, hh]
        vext = jnp.concatenate([v_all[rows_of(c), lanes_of(hh)].astype(BF16), ones_blk], axis=1)
        nd_in[c, hh] = _dot(w_un, vext)
        cn_loc[c, hh] = lax.dot_general(kw, vext, (((0,), (0,)), ((), ())),
                                        preferred_element_type=F32)

    def finish(rows):
        hg = h_ref[rows, :] * _sigmoid(oz_ref[rows, :WIDTH])
        normed = []
        for hh in range(hh_n):
            blk = hg[:, hh * dh:(hh + 1) * dh]
            normed.append(blk * lax.rsqrt(jnp.mean(blk * blk, axis=-1, keepdims=True) + RMS_EPS))
        y_d = jnp.concatenate(normed, axis=1) * mnorm_ref[...]
        p_d = _dot((y_d * _silu(oz_ref[rows, WIDTH:])).astype(BF16), wb_ref[...])
        merged = acc_ref[rows, :] + gate_ref[rows, :] * p_d
        out = _dot(merged.astype(BF16), wout_ref[...])
        out_ref[rows, :] = x_ref[rows, :] + _rms(out, npost_ref[...])

    state = [(c_ref[hh], m_ref[hh][:, 0:1]) for hh in range(hh_n)]
    for c in range(tm // ll):
        inter = [_dot(qhb[c, hh], state[hh][0].astype(BF16)) for hh in range(hh_n)]
        for hh in range(hh_n):
            b_h, m_in, _, m_loc, _, g_h = stage_b[c, hh]
            cn_prev, m_prev = state[hh]
            e_log = b_h + m_prev
            m_t = jnp.maximum(m_in, e_log)
            tot = jnp.exp(m_in - m_t) * nd_in[c, hh] + jnp.exp(e_log - m_t) * inter[hh]
            h_ref[rows_of(c), lanes_of(hh)] = tot[:, :dh] / jnp.maximum(
                jnp.abs(tot[:, dh:dh + 1]), jnp.exp(-m_t))
            m_new = jnp.maximum(g_h + m_prev, m_loc)
            state[hh] = (jnp.exp(g_h + m_prev - m_new) * cn_prev
                         + jnp.exp(m_loc - m_new) * cn_loc[c, hh], m_new)
    for hh in range(hh_n):
        c_ref[hh] = state[hh][0]
        m_ref[hh] = jnp.broadcast_to(state[hh][1], (1, dh))
    finish(slice(0, tm))


def _mlstm_call(x, hb, acc, npost, wc, woz, wg, convw, b_i, b_f, mnorm, wb, wout, *, layer, tm):
    bsz, s, _ = x.shape
    assert s % tm == 0 and tm % MLSTM_CHUNK == 0
    kern = functools.partial(_mlstm_kernel, tm=tm)
    row_spec = pl.BlockSpec((None, tm, D_MODEL), lambda b, i: (b, i, 0))
    consts = (npost, wc, woz, wg, convw, b_i, b_f, mnorm, wb, wout)
    return pl.pallas_call(
        kern,
        out_shape=jax.ShapeDtypeStruct(x.shape, F32),
        grid=(bsz, s // tm),
        in_specs=[row_spec, row_spec, row_spec] + [_layer_spec(c, layer) for c in consts],
        out_specs=row_spec,
        scratch_shapes=[pltpu.VMEM((CONV_HALO + tm, 2 * WIDTH), F32),
                        pltpu.VMEM((tm, WIDTH), F32),
                        pltpu.VMEM((MLSTM_HEADS, MLSTM_HEAD_DIM, 2 * MLSTM_HEAD_DIM), F32),
                        pltpu.VMEM((MLSTM_HEADS, 1, MLSTM_HEAD_DIM), F32),
                        pltpu.VMEM((tm, 2 * WIDTH), F32),
                        pltpu.VMEM((tm, D_MODEL), F32)],
        compiler_params=pltpu.CompilerParams(
            dimension_semantics=("arbitrary", "arbitrary"), vmem_limit_bytes=VMEM_LIMIT_BYTES),
        name="mlstm_merge_out",
    )(x, hb, acc, *consts)


def _s5_params(a_re, a_im, log_dt, b_re, b_im, c_re, c_im):
    depth = a_re.shape[0]
    lam = lax.complex(a_re.astype(F32), a_im.astype(F32))
    dt = jnp.exp(log_dt.astype(F32))[..., None]
    a_bar = jnp.exp(lam * dt)
    b_bar = ((a_bar - 1.0) / lam)[..., None] * lax.complex(b_re.astype(F32), b_im.astype(F32))
    eye = jnp.eye(S5_HALF_GROUPS, dtype=F32)
    halves = lambda m: m.reshape(depth, 2, S5_HALF_GROUPS, *m.shape[2:])

    def block_in(m):
        return jnp.einsum('lzgnp,gh->lzgphn', m, eye).reshape(depth, 2, S5_HALF_IN, S5_HALF_STATE)

    def block_out(m):
        return jnp.einsum('lzgpn,gh->lzgnhp', m, eye).reshape(depth, 2, S5_HALF_STATE, S5_HALF_IN)

    bblk = jnp.concatenate([block_in(halves(jnp.real(b_bar))), block_in(halves(jnp.imag(b_bar)))],
                           axis=3).astype(BF16)
    cblk = jnp.concatenate([block_out(halves(c_re.astype(F32))), block_out(halves(-c_im.astype(F32)))],
                           axis=2).astype(BF16)
    are = jnp.real(a_bar).reshape(depth, 1, S5_GROUPS * S5_STATE)
    aim = jnp.imag(a_bar).reshape(depth, 1, S5_GROUPS * S5_STATE)
    return bblk, cblk, are, aim


def _forward(x, norm_pre_w, norm_post_w, w_in, s5_A_re, s5_A_im, s5_log_dt, s5_B_re, s5_B_im,
             s5_C_re, s5_C_im, s5_D, s5_w_glu, pool_w, pool_scale, sconv_w, mlstm_conv_w,
             mlstm_b_i, mlstm_b_f, mlstm_norm_w, w_branch, w_out, *, tt, tm_ps, tm_ml):
    depth = w_in.shape[0]
    rows = lambda v: v.astype(F32).reshape(depth, 1, -1)
    cols = lambda a, b: w_in[:, :, a:b]
    gate_w = lambda lo, hi: cols(O_GATE + lo * D_MODEL, O_GATE + hi * D_MODEL).astype(BF16)
    pad_lanes = lambda v: jnp.pad(v.astype(F32), ((0, 0), (0, GATE_LANES - v.shape[1])))[:, None, :]
    pad_rows = lambda w: jnp.pad(w.astype(F32), ((0, 0), (0, 8 - w.shape[1]), (0, 0)))
    zero_gate = jnp.zeros((depth, D_MODEL, GATE_LANES - MLSTM_HEADS), w_in.dtype)
    npre, npost = rows(norm_pre_w), rows(norm_post_w)
    s5_consts = (npre, cols(O_S5U, O_PU).astype(BF16), gate_w(0, 1),
                 *_s5_params(s5_A_re, s5_A_im, s5_log_dt, s5_B_re, s5_B_im, s5_C_re, s5_C_im),
                 rows(s5_D), s5_w_glu.astype(BF16), w_branch[:, 0].astype(BF16))
    ps_consts = (cols(O_PU, O_QK).astype(BF16), gate_w(1, 3), pool_w.astype(BF16), rows(pool_scale),
                 pad_rows(sconv_w), w_branch[:, 1:3].astype(BF16))
    ml_consts = (npost,
                 jnp.concatenate([cols(O_QK, O_O), cols(O_I, O_F), zero_gate, cols(O_F, O_Z), zero_gate],
                                 axis=2).astype(BF16),
                 jnp.concatenate([cols(O_O, O_I), cols(O_Z, O_GATE)], axis=2).astype(BF16),
                 gate_w(3, 4), pad_rows(mlstm_conv_w), pad_lanes(mlstm_b_i), pad_lanes(mlstm_b_f),
                 rows(mlstm_norm_w), w_branch[:, 3].astype(BF16), w_out.astype(BF16))
    for l in range(depth):
        acc, hb = _s5_call(x, *s5_consts, layer=l, tt=tt)
        acc = _pool_sconv_call(hb, acc, *ps_consts, layer=l, tm=tm_ps)
        x = _mlstm_call(x, hb, acc, *ml_consts, layer=l, tm=tm_ml)
    return x


def kernel(x, norm_pre_w, norm_post_w, w_in, s5_A_re, s5_A_im, s5_log_dt, s5_B_re, s5_B_im, s5_C_re, s5_C_im, s5_D, s5_w_glu, pool_w, pool_scale, sconv_w, mlstm_conv_w, mlstm_b_i, mlstm_b_f, mlstm_norm_w, w_branch, w_out):
    return _forward(x, norm_pre_w, norm_post_w, w_in, s5_A_re, s5_A_im, s5_log_dt, s5_B_re,
                    s5_B_im, s5_C_re, s5_C_im, s5_D, s5_w_glu, pool_w, pool_scale, sconv_w,
                    mlstm_conv_w, mlstm_b_i, mlstm_b_f, mlstm_norm_w, w_branch, w_out,
                    tt=32, tm_ps=1024, tm_ml=512)
```

```python
import functools

import jax
import jax.numpy as jnp
import numpy as np
from jax import lax
from jax.experimental import pallas as pl
from jax.experimental.pallas import tpu as pltpu

F32 = jnp.float32
BF16 = jnp.bfloat16

D_MODEL = 1024
N_BRANCHES = 4
WIDTH = 512
RMS_EPS = 1e-6
S5_GROUP = 16
S5_GROUPS = WIDTH // S5_GROUP
S5_STATE = 64
S5_HALF_GROUPS = 16
S5_HALF_IN = S5_HALF_GROUPS * S5_GROUP
S5_HALF_STATE = S5_HALF_GROUPS * S5_STATE
POOL_WINDOWS = (2, 4, 8, 16)
POOL_GROUP = WIDTH // len(POOL_WINDOWS)
POOL_HALO = 32
SCONV_WIDTH = 3
CONV_HALO = 8
MLSTM_HEADS = 4
MLSTM_HEAD_DIM = WIDTH // MLSTM_HEADS
MLSTM_CONV_WIDTH = 4
MLSTM_CHUNK = 256
LANES = 128
GATE_LANES = LANES

IN_SIZES = (WIDTH, WIDTH, WIDTH, WIDTH, WIDTH, WIDTH, WIDTH, WIDTH,
            2 * WIDTH, WIDTH, WIDTH, MLSTM_HEADS, MLSTM_HEADS, WIDTH, N_BRANCHES * D_MODEL)
_OFF = np.concatenate([[0], np.cumsum(IN_SIZES)]).tolist()
(O_S5U, O_S5Z, O_PU, O_PZ, O_SCX, O_SCB, O_SCC, O_SCZ,
 O_QK, O_V, O_O, O_I, O_F, O_Z, O_GATE, O_END) = _OFF

VMEM_LIMIT_BYTES = 56 * 1024 * 1024


def _rms(x, w):
    ms = jnp.mean(x * x, axis=-1, keepdims=True)
    return x * lax.rsqrt(ms + RMS_EPS) * w


def _sigmoid(x):
    return 0.5 * jnp.tanh(0.5 * x) + 0.5


def _silu(x):
    return x * _sigmoid(x)


def _gelu_tanh(x):
    c = np.float32(np.sqrt(2.0 / np.pi))
    return x * (0.5 * (1.0 + jnp.tanh(c * (x + 0.044715 * (x * x * x)))))


def _log_sigmoid(x):
    return jnp.minimum(x, 0.0) - jnp.log1p(jnp.exp(-jnp.abs(x)))


def _dot(a, b):
    return jnp.dot(a, b, preferred_element_type=F32)


def _layer_spec(stacked, layer):
    shape = stacked.shape[1:]
    return pl.BlockSpec((None,) + shape, lambda *_: (layer,) + (0,) * len(shape),
                        pipeline_mode=pl.Buffered(1))


def _s5_kernel(x_ref, npre_ref, wuz_ref, wg_ref, bblk_ref, cblk_ref, are_ref, aim_ref,
               dskip_ref, wglu_ref, wb_ref, out_ref, hb_out_ref, bu_ref, st_ref, gate_ref, us_ref,
               ys_ref, xs_ref,
               *, nb, tt, pitch, pitch_t, scan_lanes):
    r = nb * tt

    @pl.when(pl.program_id(0) == 0)
    def _():
        st_ref[...] = jnp.zeros_like(st_ref)

    x = x_ref[...].reshape(r, D_MODEL)
    hb = _rms(x, npre_ref[...]).astype(BF16)
    hb_out_ref[...] = hb.reshape(nb, tt, D_MODEL)
    uz = _dot(hb, wuz_ref[...])
    u = uz[:, :WIDTH]
    z = uz[:, WIDTH:]

    n_u = WIDTH // LANES
    for b in range(nb):
        for k in range(n_u):
            us_ref[k, b * pitch:b * pitch + tt, :] = u[b * tt:(b + 1) * tt, k * LANES:(k + 1) * LANES]
    u_tb = jnp.concatenate(
        [jnp.concatenate([us_ref[k, pl.ds(t, nb, stride=pitch), :] for k in range(n_u)], axis=1)
         for t in range(tt)], axis=0)
    ub = u_tb.astype(BF16)

    def b_stage(half):
        cols = slice(half * 2 * S5_HALF_STATE, (half + 1) * 2 * S5_HALF_STATE)
        bu_ref[:, cols] = _dot(ub[:, half * S5_HALF_IN:(half + 1) * S5_HALF_IN], bblk_ref[half])

    def scan_stage(half):
        for c0 in range(0, S5_HALF_STATE, scan_lanes):
            re0 = half * 2 * S5_HALF_STATE + c0
            im0 = re0 + S5_HALF_STATE
            s0 = half * S5_HALF_STATE + c0
            ar = jnp.broadcast_to(are_ref[:, s0:s0 + scan_lanes], (nb, scan_lanes))
            ai = jnp.broadcast_to(aim_ref[:, s0:s0 + scan_lanes], (nb, scan_lanes))
            xr = st_ref[:, re0:re0 + scan_lanes]
            xi = st_ref[:, im0:im0 + scan_lanes]
            for t in range(tt):
                rows = slice(t * nb, (t + 1) * nb)
                nr = ar * xr - ai * xi + bu_ref[rows, re0:re0 + scan_lanes]
                ni = ar * xi + ai * xr + bu_ref[rows, im0:im0 + scan_lanes]
                xs_ref[rows, re0:re0 + scan_lanes] = nr.astype(BF16)
                xs_ref[rows, im0:im0 + scan_lanes] = ni.astype(BF16)
                xr, xi = nr, ni
            st_ref[:, re0:re0 + scan_lanes] = xr
            st_ref[:, im0:im0 + scan_lanes] = xi

    def c_stage(half):
        cols = slice(half * 2 * S5_HALF_STATE, (half + 1) * 2 * S5_HALF_STATE)
        return _dot(xs_ref[:, cols], cblk_ref[half])

    b_stage(0)
    b_stage(1)
    gate_ref[...] = _sigmoid(_dot(hb, wg_ref[...]))
    scan_stage(0)
    y0 = c_stage(0)
    scan_stage(1)
    y1 = c_stage(1)
    y_tb = jnp.concatenate([y0, y1], axis=1) + dskip_ref[...] * u_tb

    for t in range(tt):
        for k in range(n_u):
            ys_ref[k, t * pitch_t:t * pitch_t + nb, :] = y_tb[t * nb:(t + 1) * nb, k * LANES:(k + 1) * LANES]
    y = jnp.concatenate(
        [jnp.concatenate([ys_ref[k, pl.ds(b, tt, stride=pitch_t), :] for k in range(n_u)], axis=1)
         for b in range(nb)], axis=0)
    y = _gelu_tanh(y)
    y = y * _sigmoid(_dot(y.astype(BF16), wglu_ref[...]))
    pb = _dot((y * _silu(z)).astype(BF16), wb_ref[...])
    out_ref[...] = (gate_ref[...] * pb).reshape(nb, tt, D_MODEL)


def _odd_tile_pitch(n):
    return n if (n // 8) % 2 == 1 else n + 8


def _s5_call(x, npre, wuz, wg, bblk, cblk, are, aim, dskip, wglu, wb, *, layer, tt, scan_lanes=512):
    nb, s, _ = x.shape
    assert s % tt == 0 and tt % 8 == 0 and nb % 8 == 0
    pitch, pitch_t = _odd_tile_pitch(tt), _odd_tile_pitch(nb)
    kern = functools.partial(_s5_kernel, nb=nb, tt=tt, pitch=pitch, pitch_t=pitch_t,
                             scan_lanes=scan_lanes)
    row_spec = pl.BlockSpec((nb, tt, D_MODEL), lambda i: (0, i, 0))
    consts = (npre, wuz, wg, bblk, cblk, are, aim, dskip, wglu, wb)
    return pl.pallas_call(
        kern,
        out_shape=(jax.ShapeDtypeStruct(x.shape, F32), jax.ShapeDtypeStruct(x.shape, BF16)),
        grid=(s // tt,),
        in_specs=[row_spec] + [_layer_spec(c, layer) for c in consts],
        out_specs=(row_spec, row_spec),
        scratch_shapes=[pltpu.VMEM((nb * tt, 4 * S5_HALF_STATE), F32),
                        pltpu.VMEM((nb, 4 * S5_HALF_STATE), F32),
                        pltpu.VMEM((nb * tt, D_MODEL), F32),
                        pltpu.VMEM((WIDTH // LANES, nb * pitch, LANES), F32),
                        pltpu.VMEM((WIDTH // LANES, tt * pitch_t, LANES), F32),
                        pltpu.VMEM((nb * tt, 4 * S5_HALF_STATE), BF16)],
        compiler_params=pltpu.CompilerParams(
            dimension_semantics=("arbitrary",), vmem_limit_bytes=VMEM_LIMIT_BYTES),
        name="s5_branch",
    )(x, *consts)


def _pool_sconv_kernel(hb_ref, acc_ref, wp_ref, wg_ref, poolw_ref, pscale_ref, scw_ref,
                       wb_ref, out_ref, pu_ref, cx_ref, gate_ref, ps_ref, *, tm):
    s_idx = pl.program_id(1)

    @pl.when(s_idx == 0)
    def _():
        pu_ref[0:POOL_HALO, :] = jnp.zeros((POOL_HALO, WIDTH), F32)
        cx_ref[0:CONV_HALO, :] = jnp.zeros((CONV_HALO, WIDTH), F32)

    @pl.when(s_idx > 0)
    def _():
        pu_ref[0:POOL_HALO, :] = pu_ref[tm:tm + POOL_HALO, :]
        cx_ref[0:CONV_HALO, :] = cx_ref[tm:tm + CONV_HALO, :]

    hb = hb_ref[...]
    proj = _dot(hb, wp_ref[...])
    gate_ref[...] = _sigmoid(_dot(hb, wg_ref[...]))
    pool_u = proj[:, 0 * WIDTH:1 * WIDTH]
    pool_z = proj[:, 1 * WIDTH:2 * WIDTH]
    sc_x = proj[:, 2 * WIDTH:3 * WIDTH]
    sc_b = proj[:, 3 * WIDTH:4 * WIDTH]
    sc_c = proj[:, 4 * WIDTH:5 * WIDTH]
    sc_z = proj[:, 5 * WIDTH:6 * WIDTH]

    pu_ref[POOL_HALO:POOL_HALO + tm, :] = pool_u
    t_glob = s_idx * tm + lax.broadcasted_iota(jnp.int32, (tm, 1), 0)
    lo, hi = 8, POOL_HALO + tm
    ps_ref[0:lo, :] = jnp.zeros((lo, WIDTH), F32)
    cur = pu_ref[lo:hi, :] + pu_ref[lo - 1:hi - 1, :]
    level = {2: cur}
    for sh in (2, 4, 8):
        ps_ref[lo:hi, :] = cur
        cur = cur + ps_ref[lo - sh:hi - sh, :]
        level[2 * sh] = cur
    mixed = []
    for gi, win in enumerate(POOL_WINDOWS):
        lanes = slice(gi * POOL_GROUP, (gi + 1) * POOL_GROUP)
        wsum = level[win][POOL_HALO - lo:, lanes]
        cnt = jnp.minimum(t_glob + 1, win).astype(F32)
        pooled = wsum / cnt - pool_u[:, lanes]
        mixed.append(_dot(pooled.astype(BF16), poolw_ref[gi]))
    y_b = jnp.concatenate(mixed, axis=1) * pscale_ref[...]

    cx = sc_c * sc_x
    cx_ref[CONV_HALO:CONV_HALO + tm, :] = cx
    conv = scw_ref[SCONV_WIDTH - 1:SCONV_WIDTH, :] * cx
    for k in range(SCONV_WIDTH - 1):
        sh = SCONV_WIDTH - 1 - k
        conv = conv + scw_ref[k:k + 1, :] * cx_ref[CONV_HALO - sh:CONV_HALO - sh + tm, :]
    y_c = sc_b * conv

    p_b = _dot((y_b * _silu(pool_z)).astype(BF16), wb_ref[0])
    p_c = _dot((y_c * _silu(sc_z)).astype(BF16), wb_ref[1])
    gates = gate_ref[...]
    out_ref[...] = acc_ref[...] + gates[:, :D_MODEL] * p_b + gates[:, D_MODEL:] * p_c


def _pool_sconv_call(hb, acc, wp, wg, poolw, pscale, scw, wb, *, layer, tm):
    bsz, s, _ = acc.shape
    assert s % tm == 0 and tm % 8 == 0 and tm >= POOL_HALO
    kern = functools.partial(_pool_sconv_kernel, tm=tm)
    row_spec = pl.BlockSpec((None, tm, D_MODEL), lambda b, i: (b, i, 0))
    consts = (wp, wg, poolw, pscale, scw, wb)
    return pl.pallas_call(
        kern,
        out_shape=jax.ShapeDtypeStruct(acc.shape, F32),
        grid=(bsz, s // tm),
        in_specs=[row_spec, row_spec] + [_layer_spec(c, layer) for c in consts],
        out_specs=row_spec,
        scratch_shapes=[pltpu.VMEM((POOL_HALO + tm, WIDTH), F32),
                        pltpu.VMEM((CONV_HALO + tm, WIDTH), F32),
                        pltpu.VMEM((tm, 2 * D_MODEL), F32),
                        pltpu.VMEM((POOL_HALO + tm, WIDTH), F32)],
        compiler_params=pltpu.CompilerParams(
            dimension_semantics=("arbitrary", "arbitrary"), vmem_limit_bytes=VMEM_LIMIT_BYTES),
        name="pool_sconv_branch",
    )(hb, acc, *consts)


def _mlstm_kernel(x_ref, hb_ref, acc_ref, npost_ref, wc_ref, woz_ref, wg_ref, convw_ref, bi_ref,
                  bf_ref, mnorm_ref, wb_ref, wout_ref, out_ref,
                  qk_ref, h_ref, c_ref, m_ref, oz_ref, gate_ref, *, tm):
    s_idx = pl.program_id(1)
    hh_n, dh, ll = MLSTM_HEADS, MLSTM_HEAD_DIM, MLSTM_CHUNK

    @pl.when(s_idx == 0)
    def _():
        qk_ref[0:CONV_HALO, :] = jnp.zeros((CONV_HALO, 2 * WIDTH), F32)
        c_ref[...] = jnp.zeros_like(c_ref)
        m_ref[...] = jnp.zeros_like(m_ref)

    @pl.when(s_idx > 0)
    def _():
        qk_ref[0:CONV_HALO, :] = qk_ref[tm:tm + CONV_HALO, :]

    hb = hb_ref[...]
    proj = _dot(hb, wc_ref[...])
    oz_ref[...] = _dot(hb, woz_ref[...])
    v_all = proj[:, 2 * WIDTH:3 * WIDTH]
    i_log = proj[:, 3 * WIDTH:3 * WIDTH + GATE_LANES] + bi_ref[...]
    log_f = _log_sigmoid(proj[:, 3 * WIDTH + GATE_LANES:3 * WIDTH + 2 * GATE_LANES] + bf_ref[...])

    qk_ref[CONV_HALO:CONV_HALO + tm, :] = proj[:, :2 * WIDTH]
    conv = convw_ref[MLSTM_CONV_WIDTH - 1:MLSTM_CONV_WIDTH, :] * proj[:, :2 * WIDTH]
    for k in range(MLSTM_CONV_WIDTH - 1):
        sh = MLSTM_CONV_WIDTH - 1 - k
        conv = conv + convw_ref[k:k + 1, :] * qk_ref[CONV_HALO - sh:CONV_HALO - sh + tm, :]
    qk_act = _silu(conv)
    q_all = qk_act[:, :WIDTH] * np.float32(dh ** -0.5)
    k_all = qk_act[:, WIDTH:]

    row_i = lax.broadcasted_iota(jnp.int32, (ll, ll), 0)
    col_i = lax.broadcasted_iota(jnp.int32, (ll, ll), 1)
    causal = col_i <= row_i
    tri = causal.astype(BF16)

    ones_blk = (lax.broadcasted_iota(jnp.int32, (ll, dh), 1) == 0).astype(BF16)

    blocks = [(c, hh) for c in range(tm // ll) for hh in range(hh_n)]
    rows_of = lambda c: slice(c * ll, (c + 1) * ll)
    lanes_of = lambda hh: slice(hh * dh, (hh + 1) * dh)
    b_cs, s_qk, qhb = {}, {}, {}
    for c in range(tm // ll):
        lf_c = log_f[rows_of(c)]
        lf_hi = lf_c.astype(BF16)
        lf_lo = (lf_c - lf_hi.astype(F32)).astype(BF16)
        b_cs[c] = _dot(tri, lf_hi) + _dot(tri, lf_lo)
    for c, hh in blocks:
        qhb[c, hh] = q_all[rows_of(c), lanes_of(hh)].astype(BF16)
        s_qk[c, hh] = lax.dot_general(qhb[c, hh], k_all[rows_of(c), lanes_of(hh)].astype(BF16),
                                      (((1,), (1,)), ((), ())), preferred_element_type=F32)
    gate_ref[...] = _sigmoid(_dot(hb, wg_ref[...]))

    stage_b = {}
    for c in range(tm // ll):
        r_col = i_log[rows_of(c)] - b_cs[c]
        r_row = r_col.T
        g_tot = b_cs[c][ll - 1:ll, :]
        a_col = g_tot + r_col
        for hh in range(hh_n):
            b_h = b_cs[c][:, hh:hh + 1]
            r_msk = jnp.where(causal, r_row[hh:hh + 1, :], -jnp.inf)
            r_max = jnp.max(r_msk, axis=1, keepdims=True)
            m_in = b_h + r_max
            w_un = (jnp.exp(r_msk - r_max) * s_qk[c, hh]).astype(BF16)
            a_h = a_col[:, hh:hh + 1]
            m_loc = jnp.max(a_h, axis=0, keepdims=True)
            kw = (jnp.exp(a_h - m_loc) * k_all[rows_of(c), lanes_of(hh)]).astype(BF16)
            stage_b[c, hh] = (b_h, m_in, w_un, m_loc, kw, g_tot[:, hh:hh + 1])

    nd_in, cn_loc = {}, {}
    for c, hh in blocks:
        _, _, w_un, _, kw, _ = stage_b[c, hh]
        vext = jnp.concatenate([v_all[rows_of(c), lanes_of(hh)].astype(BF16), ones_blk], axis=1)
        nd_in[c, hh] = _dot(w_un, vext)
        cn_loc[c, hh] = lax.dot_general(kw, vext, (((0,), (0,)), ((), ())),
                                        preferred_element_type=F32)

    def finish(rows):
        hg = h_ref[rows, :] * _sigmoid(oz_ref[rows, :WIDTH])
        normed = []
        for hh in range(hh_n):
            blk = hg[:, hh * dh:(hh + 1) * dh]
            normed.append(blk * lax.rsqrt(jnp.mean(blk * blk, axis=-1, keepdims=True) + RMS_EPS))
        y_d = jnp.concatenate(normed, axis=1) * mnorm_ref[...]
        p_d = _dot((y_d * _silu(oz_ref[rows, WIDTH:])).astype(BF16), wb_ref[...])
        merged = acc_ref[rows, :] + gate_ref[rows, :] * p_d
        out = _dot(merged.astype(BF16), wout_ref[...])
        out_ref[rows, :] = x_ref[rows, :] + _rms(out, npost_ref[...])

    state = [(c_ref[hh], m_ref[hh][:, 0:1]) for hh in range(hh_n)]
    for c in range(tm // ll):
        inter = [_dot(qhb[c, hh], state[hh][0].astype(BF16)) for hh in range(hh_n)]
        for hh in range(hh_n):
            b_h, m_in, _, m_loc, _, g_h = stage_b[c, hh]
            cn_prev, m_prev = state[hh]
            e_log = b_h + m_prev
            m_t = jnp.maximum(m_in, e_log)
            tot = jnp.exp(m_in - m_t) * nd_in[c, hh] + jnp.exp(e_log - m_t) * inter[hh]
            h_ref[rows_of(c), lanes_of(hh)] = tot[:, :dh] / jnp.maximum(
                jnp.abs(tot[:, dh:dh + 1]), jnp.exp(-m_t))
            m_new = jnp.maximum(g_h + m_prev, m_loc)
            state[hh] = (jnp.exp(g_h + m_prev - m_new) * cn_prev
                         + jnp.exp(m_loc - m_new) * cn_loc[c, hh], m_new)
    for hh in range(hh_n):
        c_ref[hh] = state[hh][0]
        m_ref[hh] = jnp.broadcast_to(state[hh][1], (1, dh))
    finish(slice(0, tm))


def _mlstm_call(x, hb, acc, npost, wc, woz, wg, convw, b_i, b_f, mnorm, wb, wout, *, layer, tm):
    bsz, s, _ = x.shape
    assert s % tm == 0 and tm % MLSTM_CHUNK == 0
    kern = functools.partial(_mlstm_kernel, tm=tm)
    row_spec = pl.BlockSpec((None, tm, D_MODEL), lambda b, i: (b, i, 0))
    consts = (npost, wc, woz, wg, convw, b_i, b_f, mnorm, wb, wout)
    return pl.pallas_call(
        kern,
        out_shape=jax.ShapeDtypeStruct(x.shape, F32),
        grid=(bsz, s // tm),
        in_specs=[row_spec, row_spec, row_spec] + [_layer_spec(c, layer) for c in consts],
        out_specs=row_spec,
        scratch_shapes=[pltpu.VMEM((CONV_HALO + tm, 2 * WIDTH), F32),
                        pltpu.VMEM((tm, WIDTH), F32),
                        pltpu.VMEM((MLSTM_HEADS, MLSTM_HEAD_DIM, 2 * MLSTM_HEAD_DIM), F32),
                        pltpu.VMEM((MLSTM_HEADS, 1, MLSTM_HEAD_DIM), F32),
                        pltpu.VMEM((tm, 2 * WIDTH), F32),
                        pltpu.VMEM((tm, D_MODEL), F32)],
        compiler_params=pltpu.CompilerParams(
            dimension_semantics=("arbitrary", "arbitrary"), vmem_limit_bytes=VMEM_LIMIT_BYTES),
        name="mlstm_merge_out",
    )(x, hb, acc, *consts)


def _s5_params(a_re, a_im, log_dt, b_re, b_im, c_re, c_im):
    depth = a_re.shape[0]
    lam = lax.complex(a_re.astype(F32), a_im.astype(F32))
    dt = jnp.exp(log_dt.astype(F32))[..., None]
    a_bar = jnp.exp(lam * dt)
    b_bar = ((a_bar - 1.0) / lam)[..., None] * lax.complex(b_re.astype(F32), b_im.astype(F32))
    eye = jnp.eye(S5_HALF_GROUPS, dtype=F32)
    halves = lambda m: m.reshape(depth, 2, S5_HALF_GROUPS, *m.shape[2:])

    def block_in(m):
        return jnp.einsum('lzgnp,gh->lzgphn', m, eye).reshape(depth, 2, S5_HALF_IN, S5_HALF_STATE)

    def block_out(m):
        return jnp.einsum('lzgpn,gh->lzgnhp', m, eye).reshape(depth, 2, S5_HALF_STATE, S5_HALF_IN)

    bblk = jnp.concatenate([block_in(halves(jnp.real(b_bar))), block_in(halves(jnp.imag(b_bar)))],
                           axis=3).astype(BF16)
    cblk = jnp.concatenate([block_out(halves(c_re.astype(F32))), block_out(halves(-c_im.astype(F32)))],
                           axis=2).astype(BF16)
    are = jnp.real(a_bar).reshape(depth, 1, S5_GROUPS * S5_STATE)
    aim = jnp.imag(a_bar).reshape(depth, 1, S5_GROUPS * S5_STATE)
    return bblk, cblk, are, aim


def _forward(x, norm_pre_w, norm_post_w, w_in, s5_A_re, s5_A_im, s5_log_dt, s5_B_re, s5_B_im,
             s5_C_re, s5_C_im, s5_D, s5_w_glu, pool_w, pool_scale, sconv_w, mlstm_conv_w,
             mlstm_b_i, mlstm_b_f, mlstm_norm_w, w_branch, w_out, *, tt, tm_ps, tm_ml):
    depth = w_in.shape[0]
    rows = lambda v: v.astype(F32).reshape(depth, 1, -1)
    cols = lambda a, b: w_in[:, :, a:b]
    gate_w = lambda lo, hi: cols(O_GATE + lo * D_MODEL, O_GATE + hi * D_MODEL).astype(BF16)
    pad_lanes = lambda v: jnp.pad(v.astype(F32), ((0, 0), (0, GATE_LANES - v.shape[1])))[:, None, :]
    pad_rows = lambda w: jnp.pad(w.astype(F32), ((0, 0), (0, 8 - w.shape[1]), (0, 0)))
    zero_gate = jnp.zeros((depth, D_MODEL, GATE_LANES - MLSTM_HEADS), w_in.dtype)
    npre, npost = rows(norm_pre_w), rows(norm_post_w)
    s5_consts = (npre, cols(O_S5U, O_PU).astype(BF16), gate_w(0, 1),
                 *_s5_params(s5_A_re, s5_A_im, s5_log_dt, s5_B_re, s5_B_im, s5_C_re, s5_C_im),
                 rows(s5_D), s5_w_glu.astype(BF16), w_branch[:, 0].astype(BF16))
    ps_consts = (cols(O_PU, O_QK).astype(BF16), gate_w(1, 3), pool_w.astype(BF16), rows(pool_scale),
                 pad_rows(sconv_w), w_branch[:, 1:3].astype(BF16))
    ml_consts = (npost,
                 jnp.concatenate([cols(O_QK, O_O), cols(O_I, O_F), zero_gate, cols(O_F, O_Z), zero_gate],
                                 axis=2).astype(BF16),
                 jnp.concatenate([cols(O_O, O_I), cols(O_Z, O_GATE)], axis=2).astype(BF16),
                 gate_w(3, 4), pad_rows(mlstm_conv_w), pad_lanes(mlstm_b_i), pad_lanes(mlstm_b_f),
                 rows(mlstm_norm_w), w_branch[:, 3].astype(BF16), w_out.astype(BF16))
    for l in range(depth):
        acc, hb = _s5_call(x, *s5_consts, layer=l, tt=tt)
        acc = _pool_sconv_call(hb, acc, *ps_consts, layer=l, tm=tm_ps)
        x = _mlstm_call(x, hb, acc, *ml_consts, layer=l, tm=tm_ml)
    return x


def kernel(x, norm_pre_w, norm_post_w, w_in, s5_A_re, s5_A_im, s5_log_dt, s5_B_re, s5_B_im, s5_C_re, s5_C_im, s5_D, s5_w_glu, pool_w, pool_scale, sconv_w, mlstm_conv_w, mlstm_b_i, mlstm_b_f, mlstm_norm_w, w_branch, w_out):
    return _forward(x, norm_pre_w, norm_post_w, w_in, s5_A_re, s5_A_im, s5_log_dt, s5_B_re,
                    s5_B_im, s5_C_re, s5_C_im, s5_D, s5_w_glu, pool_w, pool_scale, sconv_w,
                    mlstm_conv_w, mlstm_b_i, mlstm_b_f, mlstm_norm_w, w_branch, w_out,
                    tt=32, tm_ps=1024, tm_ml=512)
```

```python
import functools

import jax
import jax.numpy as jnp
import numpy as np
from jax import lax
from jax.experimental import pallas as pl
from jax.experimental.pallas import tpu as pltpu

F32 = jnp.float32
BF16 = jnp.bfloat16

D_MODEL = 1024
N_BRANCHES = 4
WIDTH = 512
RMS_EPS = 1e-6
S5_GROUP = 16
S5_GROUPS = WIDTH // S5_GROUP
S5_STATE = 64
S5_HALF_GROUPS = 16
S5_HALF_IN = S5_HALF_GROUPS * S5_GROUP
S5_HALF_STATE = S5_HALF_GROUPS * S5_STATE
POOL_WINDOWS = (2, 4, 8, 16)
POOL_GROUP = WIDTH // len(POOL_WINDOWS)
POOL_HALO = 32
SCONV_WIDTH = 3
CONV_HALO = 8
MLSTM_HEADS = 4
MLSTM_HEAD_DIM = WIDTH // MLSTM_HEADS
MLSTM_CONV_WIDTH = 4
MLSTM_CHUNK = 256
LANES = 128
GATE_LANES = LANES

IN_SIZES = (WIDTH, WIDTH, WIDTH, WIDTH, WIDTH, WIDTH, WIDTH, WIDTH,
            2 * WIDTH, WIDTH, WIDTH, MLSTM_HEADS, MLSTM_HEADS, WIDTH, N_BRANCHES * D_MODEL)
_OFF = np.concatenate([[0], np.cumsum(IN_SIZES)]).tolist()
(O_S5U, O_S5Z, O_PU, O_PZ, O_SCX, O_SCB, O_SCC, O_SCZ,
 O_QK, O_V, O_O, O_I, O_F, O_Z, O_GATE, O_END) = _OFF

VMEM_LIMIT_BYTES = 56 * 1024 * 1024


def _rms(x, w):
    ms = jnp.mean(x * x, axis=-1, keepdims=True)
    return x * lax.rsqrt(ms + RMS_EPS) * w


def _sigmoid(x):
    return 0.5 * jnp.tanh(0.5 * x) + 0.5


def _silu(x):
    return x * _sigmoid(x)


def _gelu_tanh(x):
    c = np.float32(np.sqrt(2.0 / np.pi))
    return x * (0.5 * (1.0 + jnp.tanh(c * (x + 0.044715 * (x * x * x)))))


def _log_sigmoid(x):
    return jnp.minimum(x, 0.0) - jnp.log1p(jnp.exp(-jnp.abs(x)))


def _dot(a, b):
    return jnp.dot(a, b, preferred_element_type=F32)


def _layer_spec(stacked, layer):
    shape = stacked.shape[1:]
    return pl.BlockSpec((None,) + shape, lambda *_: (layer,) + (0,) * len(shape),
                        pipeline_mode=pl.Buffered(1))


def _s5_kernel(x_ref, npre_ref, wuz_ref, wg_ref, bblk_ref, cblk_ref, are_ref, aim_ref,
               dskip_ref, wglu_ref, wb_ref, out_ref, hb_out_ref, bu_ref, st_ref, gate_ref, us_ref,
               ys_ref, xs_ref,
               *, nb, tt, pitch, pitch_t, scan_lanes):
    r = nb * tt

    @pl.when(pl.program_id(0) == 0)
    def _():
        st_ref[...] = jnp.zeros_like(st_ref)

    x = x_ref[...].reshape(r, D_MODEL)
    hb = _rms(x, npre_ref[...]).astype(BF16)
    hb_out_ref[...] = hb.reshape(nb, tt, D_MODEL)
    uz = _dot(hb, wuz_ref[...])
    u = uz[:, :WIDTH]
    z = uz[:, WIDTH:]

    n_u = WIDTH // LANES
    for b in range(nb):
        for k in range(n_u):
            us_ref[k, b * pitch:b * pitch + tt, :] = u[b * tt:(b + 1) * tt, k * LANES:(k + 1) * LANES]
    u_tb = jnp.concatenate(
        [jnp.concatenate([us_ref[k, pl.ds(t, nb, stride=pitch), :] for k in range(n_u)], axis=1)
         for t in range(tt)], axis=0)
    ub = u_tb.astype(BF16)

    def b_stage(half):
        cols = slice(half * 2 * S5_HALF_STATE, (half + 1) * 2 * S5_HALF_STATE)
        bu_ref[:, cols] = _dot(ub[:, half * S5_HALF_IN:(half + 1) * S5_HALF_IN], bblk_ref[half])

    def scan_stage(half):
        for c0 in range(0, S5_HALF_STATE, scan_lanes):
            re0 = half * 2 * S5_HALF_STATE + c0
            im0 = re0 + S5_HALF_STATE
            s0 = half * S5_HALF_STATE + c0
            ar = jnp.broadcast_to(are_ref[:, s0:s0 + scan_lanes], (nb, scan_lanes))
            ai = jnp.broadcast_to(aim_ref[:, s0:s0 + scan_lanes], (nb, scan_lanes))
            xr = st_ref[:, re0:re0 + scan_lanes]
            xi = st_ref[:, im0:im0 + scan_lanes]
            for t in range(tt):
                rows = slice(t * nb, (t + 1) * nb)
                nr = ar * xr - ai * xi + bu_ref[rows, re0:re0 + scan_lanes]
                ni = ar * xi + ai * xr + bu_ref[rows, im0:im0 + scan_lanes]
                xs_ref[rows, re0:re0 + scan_lanes] = nr.astype(BF16)
                xs_ref[rows, im0:im0 + scan_lanes] = ni.astype(BF16)
                xr, xi = nr, ni
            st_ref[:, re0:re0 + scan_lanes] = xr
            st_ref[:, im0:im0 + scan_lanes] = xi

    def c_stage(half):
        cols = slice(half * 2 * S5_HALF_STATE, (half + 1) * 2 * S5_HALF_STATE)
        return _dot(xs_ref[:, cols], cblk_ref[half])

    b_stage(0)
    b_stage(1)
    gate_ref[...] = _sigmoid(_dot(hb, wg_ref[...]))
    scan_stage(0)
    y0 = c_stage(0)
    scan_stage(1)
    y1 = c_stage(1)
    y_tb = jnp.concatenate([y0, y1], axis=1) + dskip_ref[...] * u_tb

    for t in range(tt):
        for k in range(n_u):
            ys_ref[k, t * pitch_t:t * pitch_t + nb, :] = y_tb[t * nb:(t + 1) * nb, k * LANES:(k + 1) * LANES]
    y = jnp.concatenate(
        [jnp.concatenate([ys_ref[k, pl.ds(b, tt, stride=pitch_t), :] for k in range(n_u)], axis=1)
         for b in range(nb)], axis=0)
    y = _gelu_tanh(y)
    y = y * _sigmoid(_dot(y.astype(BF16), wglu_ref[...]))
    pb = _dot((y * _silu(z)).astype(BF16), wb_ref[...])
    out_ref[...] = (gate_ref[...] * pb).reshape(nb, tt, D_MODEL)


def _odd_tile_pitch(n):
    return n if (n // 8) % 2 == 1 else n + 8


def _s5_call(x, npre, wuz, wg, bblk, cblk, are, aim, dskip, wglu, wb, *, layer, tt, scan_lanes=512):
    nb, s, _ = x.shape
    assert s % tt == 0 and tt % 8 == 0 and nb % 8 == 0
    pitch, pitch_t = _odd_tile_pitch(tt), _odd_tile_pitch(nb)
    kern = functools.partial(_s5_kernel, nb=nb, tt=tt, pitch=pitch, pitch_t=pitch_t,
                             scan_lanes=scan_lanes)
    row_spec = pl.BlockSpec((nb, tt, D_MODEL), lambda i: (0, i, 0))
    consts = (npre, wuz, wg, bblk, cblk, are, aim, dskip, wglu, wb)
    return pl.pallas_call(
        kern,
        out_shape=(jax.ShapeDtypeStruct(x.shape, F32), jax.ShapeDtypeStruct(x.shape, BF16)),
        grid=(s // tt,),
        in_specs=[row_spec] + [_layer_spec(c, layer) for c in consts],
        out_specs=(row_spec, row_spec),
        scratch_shapes=[pltpu.VMEM((nb * tt, 4 * S5_HALF_STATE), F32),
                        pltpu.VMEM((nb, 4 * S5_HALF_STATE), F32),
                        pltpu.VMEM((nb * tt, D_MODEL), F32),
                        pltpu.VMEM((WIDTH // LANES, nb * pitch, LANES), F32),
                        pltpu.VMEM((WIDTH // LANES, tt * pitch_t, LANES), F32),
                        pltpu.VMEM((nb * tt, 4 * S5_HALF_STATE), BF16)],
        compiler_params=pltpu.CompilerParams(
            dimension_semantics=("arbitrary",), vmem_limit_bytes=VMEM_LIMIT_BYTES),
        name="s5_branch",
    )(x, *consts)


def _pool_sconv_kernel(hb_ref, acc_ref, wp_ref, wg_ref, poolw_ref, pscale_ref, scw_ref,
                       wb_ref, out_ref, pu_ref, cx_ref, gate_ref, ps_ref, *, tm):
    s_idx = pl.program_id(1)

    @pl.when(s_idx == 0)
    def _():
        pu_ref[0:POOL_HALO, :] = jnp.zeros((POOL_HALO, WIDTH), F32)
        cx_ref[0:CONV_HALO, :] = jnp.zeros((CONV_HALO, WIDTH), F32)

    @pl.when(s_idx > 0)
    def _():
        pu_ref[0:POOL_HALO, :] = pu_ref[tm:tm + POOL_HALO, :]
        cx_ref[0:CONV_HALO, :] = cx_ref[tm:tm + CONV_HALO, :]

    hb = hb_ref[...]
    proj = _dot(hb, wp_ref[...])
    gate_ref[...] = _sigmoid(_dot(hb, wg_ref[...]))
    pool_u = proj[:, 0 * WIDTH:1 * WIDTH]
    pool_z = proj[:, 1 * WIDTH:2 * WIDTH]
    sc_x = proj[:, 2 * WIDTH:3 * WIDTH]
    sc_b = proj[:, 3 * WIDTH:4 * WIDTH]
    sc_c = proj[:, 4 * WIDTH:5 * WIDTH]
    sc_z = proj[:, 5 * WIDTH:6 * WIDTH]

    pu_ref[POOL_HALO:POOL_HALO + tm, :] = pool_u
    t_glob = s_idx * tm + lax.broadcasted_iota(jnp.int32, (tm, 1), 0)
    lo, hi = 8, POOL_HALO + tm
    ps_ref[0:lo, :] = jnp.zeros((lo, WIDTH), F32)
    cur = pu_ref[lo:hi, :] + pu_ref[lo - 1:hi - 1, :]
    level = {2: cur}
    for sh in (2, 4, 8):
        ps_ref[lo:hi, :] = cur
        cur = cur + ps_ref[lo - sh:hi - sh, :]
        level[2 * sh] = cur
    mixed = []
    for gi, win in enumerate(POOL_WINDOWS):
        lanes = slice(gi * POOL_GROUP, (gi + 1) * POOL_GROUP)
        wsum = level[win][POOL_HALO - lo:, lanes]
        cnt = jnp.minimum(t_glob + 1, win).astype(F32)
        pooled = wsum / cnt - pool_u[:, lanes]
        mixed.append(_dot(pooled.astype(BF16), poolw_ref[gi]))
    y_b = jnp.concatenate(mixed, axis=1) * pscale_ref[...]

    cx = sc_c * sc_x
    cx_ref[CONV_HALO:CONV_HALO + tm, :] = cx
    conv = scw_ref[SCONV_WIDTH - 1:SCONV_WIDTH, :] * cx
    for k in range(SCONV_WIDTH - 1):
        sh = SCONV_WIDTH - 1 - k
        conv = conv + scw_ref[k:k + 1, :] * cx_ref[CONV_HALO - sh:CONV_HALO - sh + tm, :]
    y_c = sc_b * conv

    p_b = _dot((y_b * _silu(pool_z)).astype(BF16), wb_ref[0])
    p_c = _dot((y_c * _silu(sc_z)).astype(BF16), wb_ref[1])
    gates = gate_ref[...]
    out_ref[...] = acc_ref[...] + gates[:, :D_MODEL] * p_b + gates[:, D_MODEL:] * p_c


def _pool_sconv_call(hb, acc, wp, wg, poolw, pscale, scw, wb, *, layer, tm):
    bsz, s, _ = acc.shape
    assert s % tm == 0 and tm % 8 == 0 and tm >= POOL_HALO
    kern = functools.partial(_pool_sconv_kernel, tm=tm)
    row_spec = pl.BlockSpec((None, tm, D_MODEL), lambda b, i: (b, i, 0))
    consts = (wp, wg, poolw, pscale, scw, wb)
    return pl.pallas_call(
        kern,
        out_shape=jax.ShapeDtypeStruct(acc.shape, F32),
        grid=(bsz, s // tm),
        in_specs=[row_spec, row_spec] + [_layer_spec(c, layer) for c in consts],
        out_specs=row_spec,
        scratch_shapes=[pltpu.VMEM((POOL_HALO + tm, WIDTH), F32),
                        pltpu.VMEM((CONV_HALO + tm, WIDTH), F32),
                        pltpu.VMEM((tm, 2 * D_MODEL), F32),
                        pltpu.VMEM((POOL_HALO + tm, WIDTH), F32)],
        compiler_params=pltpu.CompilerParams(
            dimension_semantics=("arbitrary", "arbitrary"), vmem_limit_bytes=VMEM_LIMIT_BYTES),
        name="pool_sconv_branch",
    )(hb, acc, *consts)


def _mlstm_kernel(x_ref, hb_ref, acc_ref, npost_ref, wc_ref, woz_ref, wg_ref, convw_ref, bi_ref,
                  bf_ref, mnorm_ref, wb_ref, wout_ref, out_ref,
                  qk_ref, h_ref, c_ref, m_ref, oz_ref, gate_ref, *, tm):
    s_idx = pl.program_id(1)
    hh_n, dh, ll = MLSTM_HEADS, MLSTM_HEAD_DIM, MLSTM_CHUNK

    @pl.when(s_idx == 0)
    def _():
        qk_ref[0:CONV_HALO, :] = jnp.zeros((CONV_HALO, 2 * WIDTH), F32)
        c_ref[...] = jnp.zeros_like(c_ref)
        m_ref[...] = jnp.zeros_like(m_ref)

    @pl.when(s_idx > 0)
    def _():
        qk_ref[0:CONV_HALO, :] = qk_ref[tm:tm + CONV_HALO, :]

    hb = hb_ref[...]
    proj = _dot(hb, wc_ref[...])
    oz_ref[...] = _dot(hb, woz_ref[...])
    v_all = proj[:, 2 * WIDTH:3 * WIDTH]
    i_log = proj[:, 3 * WIDTH:3 * WIDTH + GATE_LANES] + bi_ref[...]
    log_f = _log_sigmoid(proj[:, 3 * WIDTH + GATE_LANES:3 * WIDTH + 2 * GATE_LANES] + bf_ref[...])

    qk_ref[CONV_HALO:CONV_HALO + tm, :] = proj[:, :2 * WIDTH]
    conv = convw_ref[MLSTM_CONV_WIDTH - 1:MLSTM_CONV_WIDTH, :] * proj[:, :2 * WIDTH]
    for k in range(MLSTM_CONV_WIDTH - 1):
        sh = MLSTM_CONV_WIDTH - 1 - k
        conv = conv + convw_ref[k:k + 1, :] * qk_ref[CONV_HALO - sh:CONV_HALO - sh + tm, :]
    qk_act = _silu(conv)
    q_all = qk_act[:, :WIDTH] * np.float32(dh ** -0.5)
    k_all = qk_act[:, WIDTH:]

    row_i = lax.broadcasted_iota(jnp.int32, (ll, ll), 0)
    col_i = lax.broadcasted_iota(jnp.int32, (ll, ll), 1)
    causal = col_i <= row_i
    tri = causal.astype(BF16)

    ones_blk = (lax.broadcasted_iota(jnp.int32, (ll, dh), 1) == 0).astype(BF16)

    blocks = [(c, hh) for c in range(tm // ll) for hh in range(hh_n)]
    rows_of = lambda c: slice(c * ll, (c + 1) * ll)
    lanes_of = lambda hh: slice(hh * dh, (hh + 1) * dh)
    b_cs, s_qk, qhb = {}, {}, {}
    for c in range(tm // ll):
        lf_c = log_f[rows_of(c)]
        lf_hi = lf_c.astype(BF16)
        lf_lo = (lf_c - lf_hi.astype(F32)).astype(BF16)
        b_cs[c] = _dot(tri, lf_hi) + _dot(tri, lf_lo)
    for c, hh in blocks:
        qhb[c, hh] = q_all[rows_of(c), lanes_of(hh)].astype(BF16)
        s_qk[c, hh] = lax.dot_general(qhb[c, hh], k_all[rows_of(c), lanes_of(hh)].astype(BF16),
                                      (((1,), (1,)), ((), ())), preferred_element_type=F32)
    gate_ref[...] = _sigmoid(_dot(hb, wg_ref[...]))

    stage_b = {}
    for c in range(tm // ll):
        r_col = i_log[rows_of(c)] - b_cs[c]
        r_row = r_col.T
        g_tot = b_cs[c][ll - 1:ll, :]
        a_col = g_tot + r_col
        for hh in range(hh_n):
            b_h = b_cs[c][:, hh:hh + 1]
            hl = ll // 2
            r_h = r_row[hh:hh + 1, :]
            top = jnp.where(causal[:hl, :hl], r_h[:, :hl], -jnp.inf)
            bot = jnp.where(causal[hl:, :], r_h, -jnp.inf)
            top_max = jnp.max(top, axis=1, keepdims=True)
            bot_max = jnp.max(bot, axis=1, keepdims=True)
            m_in = b_h + jnp.concatenate([top_max, bot_max], axis=0)
            w_top = jnp.exp(top - top_max) * s_qk[c, hh][:hl, :hl]
            w_bot = jnp.exp(bot - bot_max) * s_qk[c, hh][hl:, :]
            w_un = (w_top.astype(BF16), w_bot.astype(BF16))
            a_h = a_col[:, hh:hh + 1]
            m_loc = jnp.max(a_h, axis=0, keepdims=True)
            kw = (jnp.exp(a_h - m_loc) * k_all[rows_of(c), lanes_of(hh)]).astype(BF16)
            stage_b[c, hh] = (b_h, m_in, w_un, m_loc, kw, g_tot[:, hh:hh + 1])

    nd_in, cn_loc = {}, {}
    for c, hh in blocks:
        _, _, w_un, _, kw, _ = stage_b[c, hh]
        vext = jnp.concatenate([v_all[rows_of(c), lanes_of(hh)].astype(BF16), ones_blk], axis=1)
        nd_in[c, hh] = jnp.concatenate([_dot(w_un[0], vext[:ll // 2]), _dot(w_un[1], vext)], axis=0)
        cn_loc[c, hh] = lax.dot_general(kw, vext, (((0,), (0,)), ((), ())),
                                        preferred_element_type=F32)

    def finish(rows):
        hg = h_ref[rows, :] * _sigmoid(oz_ref[rows, :WIDTH])
        normed = []
        for hh in range(hh_n):
            blk = hg[:, hh * dh:(hh + 1) * dh]
            normed.append(blk * lax.rsqrt(jnp.mean(blk * blk, axis=-1, keepdims=True) + RMS_EPS))
        y_d = jnp.concatenate(normed, axis=1) * mnorm_ref[...]
        p_d = _dot((y_d * _silu(oz_ref[rows, WIDTH:])).astype(BF16), wb_ref[...])
        merged = acc_ref[rows, :] + gate_ref[rows, :] * p_d
        out = _dot(merged.astype(BF16), wout_ref[...])
        out_ref[rows, :] = x_ref[rows, :] + _rms(out, npost_ref[...])

    state = [(c_ref[hh], m_ref[hh][:, 0:1]) for hh in range(hh_n)]
    for c in range(tm // ll):
        inter = [_dot(qhb[c, hh], state[hh][0].astype(BF16)) for hh in range(hh_n)]
        for hh in range(hh_n):
            b_h, m_in, _, m_loc, _, g_h = stage_b[c, hh]
            cn_prev, m_prev = state[hh]
            e_log = b_h + m_prev
            m_t = jnp.maximum(m_in, e_log)
            tot = jnp.exp(m_in - m_t) * nd_in[c, hh] + jnp.exp(e_log - m_t) * inter[hh]
            h_ref[rows_of(c), lanes_of(hh)] = tot[:, :dh] / jnp.maximum(
                jnp.abs(tot[:, dh:dh + 1]), jnp.exp(-m_t))
            m_new = jnp.maximum(g_h + m_prev, m_loc)
            state[hh] = (jnp.exp(g_h + m_prev - m_new) * cn_prev
                         + jnp.exp(m_loc - m_new) * cn_loc[c, hh], m_new)
    for hh in range(hh_n):
        c_ref[hh] = state[hh][0]
        m_ref[hh] = jnp.broadcast_to(state[hh][1], (1, dh))
    finish(slice(0, tm))


def _mlstm_call(x, hb, acc, npost, wc, woz, wg, convw, b_i, b_f, mnorm, wb, wout, *, layer, tm):
    bsz, s, _ = x.shape
    assert s % tm == 0 and tm % MLSTM_CHUNK == 0
    kern = functools.partial(_mlstm_kernel, tm=tm)
    row_spec = pl.BlockSpec((None, tm, D_MODEL), lambda b, i: (b, i, 0))
    consts = (npost, wc, woz, wg, convw, b_i, b_f, mnorm, wb, wout)
    return pl.pallas_call(
        kern,
        out_shape=jax.ShapeDtypeStruct(x.shape, F32),
        grid=(bsz, s // tm),
        in_specs=[row_spec, row_spec, row_spec] + [_layer_spec(c, layer) for c in consts],
        out_specs=row_spec,
        scratch_shapes=[pltpu.VMEM((CONV_HALO + tm, 2 * WIDTH), F32),
                        pltpu.VMEM((tm, WIDTH), F32),
                        pltpu.VMEM((MLSTM_HEADS, MLSTM_HEAD_DIM, 2 * MLSTM_HEAD_DIM), F32),
                        pltpu.VMEM((MLSTM_HEADS, 1, MLSTM_HEAD_DIM), F32),
                        pltpu.VMEM((tm, 2 * WIDTH), F32),
                        pltpu.VMEM((tm, D_MODEL), F32)],
        compiler_params=pltpu.CompilerParams(
            dimension_semantics=("arbitrary", "arbitrary"), vmem_limit_bytes=VMEM_LIMIT_BYTES),
        name="mlstm_merge_out",
    )(x, hb, acc, *consts)


def _s5_params(a_re, a_im, log_dt, b_re, b_im, c_re, c_im):
    depth = a_re.shape[0]
    lam = lax.complex(a_re.astype(F32), a_im.astype(F32))
    dt = jnp.exp(log_dt.astype(F32))[..., None]
    a_bar = jnp.exp(lam * dt)
    b_bar = ((a_bar - 1.0) / lam)[..., None] * lax.complex(b_re.astype(F32), b_im.astype(F32))
    eye = jnp.eye(S5_HALF_GROUPS, dtype=F32)
    halves = lambda m: m.reshape(depth, 2, S5_HALF_GROUPS, *m.shape[2:])

    def block_in(m):
        return jnp.einsum('lzgnp,gh->lzgphn', m, eye).reshape(depth, 2, S5_HALF_IN, S5_HALF_STATE)

    def block_out(m):
        return jnp.einsum('lzgpn,gh->lzgnhp', m, eye).reshape(depth, 2, S5_HALF_STATE, S5_HALF_IN)

    bblk = jnp.concatenate([block_in(halves(jnp.real(b_bar))), block_in(halves(jnp.imag(b_bar)))],
                           axis=3).astype(BF16)
    cblk = jnp.concatenate([block_out(halves(c_re.astype(F32))), block_out(halves(-c_im.astype(F32)))],
                           axis=2).astype(BF16)
    are = jnp.real(a_bar).reshape(depth, 1, S5_GROUPS * S5_STATE)
    aim = jnp.imag(a_bar).reshape(depth, 1, S5_GROUPS * S5_STATE)
    return bblk, cblk, are, aim


def _forward(x, norm_pre_w, norm_post_w, w_in, s5_A_re, s5_A_im, s5_log_dt, s5_B_re, s5_B_im,
             s5_C_re, s5_C_im, s5_D, s5_w_glu, pool_w, pool_scale, sconv_w, mlstm_conv_w,
             mlstm_b_i, mlstm_b_f, mlstm_norm_w, w_branch, w_out, *, tt, tm_ps, tm_ml):
    depth = w_in.shape[0]
    rows = lambda v: v.astype(F32).reshape(depth, 1, -1)
    cols = lambda a, b: w_in[:, :, a:b]
    gate_w = lambda lo, hi: cols(O_GATE + lo * D_MODEL, O_GATE + hi * D_MODEL).astype(BF16)
    pad_lanes = lambda v: jnp.pad(v.astype(F32), ((0, 0), (0, GATE_LANES - v.shape[1])))[:, None, :]
    pad_rows = lambda w: jnp.pad(w.astype(F32), ((0, 0), (0, 8 - w.shape[1]), (0, 0)))
    zero_gate = jnp.zeros((depth, D_MODEL, GATE_LANES - MLSTM_HEADS), w_in.dtype)
    npre, npost = rows(norm_pre_w), rows(norm_post_w)
    s5_consts = (npre, cols(O_S5U, O_PU).astype(BF16), gate_w(0, 1),
                 *_s5_params(s5_A_re, s5_A_im, s5_log_dt, s5_B_re, s5_B_im, s5_C_re, s5_C_im),
                 rows(s5_D), s5_w_glu.astype(BF16), w_branch[:, 0].astype(BF16))
    ps_consts = (cols(O_PU, O_QK).astype(BF16), gate_w(1, 3), pool_w.astype(BF16), rows(pool_scale),
                 pad_rows(sconv_w), w_branch[:, 1:3].astype(BF16))
    ml_consts = (npost,
                 jnp.concatenate([cols(O_QK, O_O), cols(O_I, O_F), zero_gate, cols(O_F, O_Z), zero_gate],
                                 axis=2).astype(BF16),
                 jnp.concatenate([cols(O_O, O_I), cols(O_Z, O_GATE)], axis=2).astype(BF16),
                 gate_w(3, 4), pad_rows(mlstm_conv_w), pad_lanes(mlstm_b_i), pad_lanes(mlstm_b_f),
                 rows(mlstm_norm_w), w_branch[:, 3].astype(BF16), w_out.astype(BF16))
    for l in range(depth):
        acc, hb = _s5_call(x, *s5_consts, layer=l, tt=tt)
        acc = _pool_sconv_call(hb, acc, *ps_consts, layer=l, tm=tm_ps)
        x = _mlstm_call(x, hb, acc, *ml_consts, layer=l, tm=tm_ml)
    return x


def kernel(x, norm_pre_w, norm_post_w, w_in, s5_A_re, s5_A_im, s5_log_dt, s5_B_re, s5_B_im, s5_C_re, s5_C_im, s5_D, s5_w_glu, pool_w, pool_scale, sconv_w, mlstm_conv_w, mlstm_b_i, mlstm_b_f, mlstm_norm_w, w_branch, w_out):
    return _forward(x, norm_pre_w, norm_post_w, w_in, s5_A_re, s5_A_im, s5_log_dt, s5_B_re,
                    s5_B_im, s5_C_re, s5_C_im, s5_D, s5_w_glu, pool_w, pool_scale, sconv_w,
                    mlstm_conv_w, mlstm_b_i, mlstm_b_f, mlstm_norm_w, w_branch, w_out,
                    tt=32, tm_ps=1024, tm_ml=512)
```
